```python
import jax, jax.numpy as jnp
from jax import lax
import numpy as np

D_MODEL = 1024
BATCH = 16
SEQ = 2048
DEPTH = 2

CHUNK = 64
NORM_EPS = 1e-6
ROPE_THETA = 10000.0

RWKV_HEADS = 8
RWKV_HEAD_DIM = 64
RWKV_WIDTH = RWKV_HEADS * RWKV_HEAD_DIM
DECAY_LORA = 64
A_LORA = 64
GATE_LORA = 128
GN_EPS = 64e-5
LRU_HEADS = 8
LRU_WIDTH = D_MODEL // 2
LRU_BLOCK = LRU_WIDTH // LRU_HEADS
CONV_WIDTH = 4
LRU_C = 8.0
MIX_WIDTH = RWKV_WIDTH + LRU_WIDTH
RWKV_COLS = (RWKV_WIDTH, RWKV_WIDTH, RWKV_WIDTH, DECAY_LORA, A_LORA, GATE_LORA)
RWKV_PROJ = sum(RWKV_COLS)
IN0_COLS = RWKV_COLS + (LRU_WIDTH, LRU_WIDTH)
IN0_WIDTH = sum(IN0_COLS)

ATT_HEADS = 16
ATT_HEAD_DIM = 64
ATT_WIDTH = ATT_HEADS * ATT_HEAD_DIM
IDX_HEADS = 8
IDX_HEAD_DIM = 64
TOPK_MAX = 256
Q_BLOCK = 128
IN1_COLS = (ATT_WIDTH, ATT_HEAD_DIM, ATT_HEAD_DIM, IDX_HEADS * IDX_HEAD_DIM, IDX_HEAD_DIM, IDX_HEADS)
IN1_WIDTH = sum(IN1_COLS)

FFN_HIDDEN = -(-8 * D_MODEL // (3 * 256)) * 256
N_EVEN = (DEPTH + 1) // 2
N_ODD = DEPTH // 2

kernel_name = 'hybrid_rwkv7_rglru_dsa_encoder'


def split_cols(t, sizes):
    return jnp.split(t, np.cumsum(sizes)[:-1].tolist(), axis=-1)


def rmsnorm(x, g):
    xf = x.astype(jnp.float32)
    y = xf * lax.rsqrt(jnp.mean(xf * xf, axis=-1, keepdims=True) + NORM_EPS)
    return (y * g.astype(jnp.float32)).astype(x.dtype)


def rope_tables(seq, dim):
    inv_freq = ROPE_THETA ** (-jnp.arange(0, dim, 2, dtype=jnp.float32) / dim)
    ang = jnp.arange(seq, dtype=jnp.float32)[:, None] * inv_freq[None, :]
    return jnp.cos(ang), jnp.sin(ang)


def apply_rope(t, cos, sin):
    tf = t.astype(jnp.float32)
    t1, t2 = jnp.split(tf, 2, axis=-1)
    c, s = cos[None, :, None, :], sin[None, :, None, :]
    return jnp.concatenate([t1 * c - t2 * s, t1 * s + t2 * c], axis=-1).astype(t.dtype)


def swiglu(h, w_gate, w_up, w_down):
    return (jax.nn.silu(h @ w_gate) * (h @ w_up)) @ w_down


def token_shift(p, mu):
    prev = jnp.pad(p, ((0, 0), (1, 0), (0, 0)))[:, :-1]
    return p + (prev - p) * mu


def rwkv7_scan(r, decay, k, v, kk, a):
    B, _, H, N = r.shape

    def step(state, inp):
        r_t, w_t, k_t, v_t, kk_t, a_t = inp
        sa = jnp.einsum('bhvk,bhk->bhv', state, -kk_t)
        state = (state * w_t[:, :, None, :]
                 + sa[..., None] * (kk_t * a_t)[:, :, None, :]
                 + v_t[..., None] * k_t[:, :, None, :])
        return state, jnp.einsum('bhvk,bhk->bhv', state, r_t)

    xs = tuple(jnp.moveaxis(t, 1, 0) for t in (r, decay, k, v, kk, a))
    _, ys = lax.scan(step, jnp.zeros((B, H, N, N), jnp.float32), xs)
    return jnp.moveaxis(ys, 0, 1)


def linear_scan_combine(c1, c2):
    a1, b1 = c1
    a2, b2 = c2
    return a1 * a2, a2 * b1 + b2


def rwkv_lru_mixer(h, w_in, mu_shift, w_decay0, w_decay_up, a_bias, w_a_up, w_g_up, k_k, k_a, r_k,
                   lnx_w, lnx_b, conv_w, conv_b, w_rgate, b_rgate, w_igate, b_igate, lru_lambda, w_out):
    B, S, _ = h.shape
    f32 = jnp.float32
    proj = h @ w_in
    rwkv_p, lru_x, lru_gate = split_cols(proj, (RWKV_PROJ, LRU_WIDTH, LRU_WIDTH))

    r, k, v, wd, ad, gd = split_cols(token_shift(rwkv_p, mu_shift), RWKV_COLS)
    heads = lambda t: t.reshape(B, S, RWKV_HEADS, RWKV_HEAD_DIM)
    w_log = -jax.nn.softplus(-(w_decay0 + jnp.tanh(wd) @ w_decay_up).astype(f32)) - 0.5
    decay = jnp.exp(-jnp.exp(w_log))
    a = jax.nn.sigmoid((a_bias + ad @ w_a_up).astype(f32))
    g = jax.nn.sigmoid(gd) @ w_g_up
    rf, kf, vf = r.astype(f32), k.astype(f32), v.astype(f32)
    kk = heads(kf * k_k)
    kk = kk * lax.rsqrt(jnp.maximum(jnp.sum(kk * kk, axis=-1, keepdims=True), 1e-24))
    k_mod = heads(kf * (1.0 + (a - 1.0) * k_a))
    y = rwkv7_scan(heads(rf), heads(decay), k_mod, heads(vf), kk, heads(a))
    y_mean = jnp.mean(y, axis=-1, keepdims=True)
    y_var = jnp.mean(jnp.square(y - y_mean), axis=-1, keepdims=True)
    y = ((y - y_mean) * lax.rsqrt(y_var + GN_EPS)).reshape(B, S, RWKV_WIDTH) * lnx_w + lnx_b
    bonus = jnp.sum(heads(rf) * k_mod * r_k, axis=-1, keepdims=True) * heads(vf)
    rwkv_out = (y + bonus.reshape(B, S, RWKV_WIDTH)) * g

    xpad = jnp.pad(lru_x, ((0, 0), (CONV_WIDTH - 1, 0), (0, 0)))
    xc = conv_b + xpad[:, 0:S] * conv_w[0]
    for j in range(1, CONV_WIDTH):
        xc = xc + xpad[:, j:j + S] * conv_w[j]
    xh = xc.reshape(B, S, LRU_HEADS, LRU_BLOCK)
    rg = jax.nn.sigmoid((jnp.einsum('bshi,hij->bshj', xh, w_rgate) + b_rgate).astype(f32))
    ig = jax.nn.sigmoid((jnp.einsum('bshi,hij->bshj', xh, w_igate) + b_igate).astype(f32))
    log_a = -LRU_C * rg * jax.nn.softplus(-lru_lambda.astype(f32)).reshape(LRU_HEADS, LRU_BLOCK)
    a_t = jnp.exp(log_a)
    x_in = xh.astype(f32) * ig * jnp.sqrt(-jnp.expm1(2.0 * log_a))
    _, hs = lax.associative_scan(linear_scan_combine, (a_t, x_in), axis=1)
    lru_out = hs.reshape(B, S, LRU_WIDTH) * jax.nn.gelu(lru_gate.astype(f32))

    mixed = jnp.concatenate([rwkv_out, lru_out], axis=-1).astype(h.dtype)
    return mixed @ w_out


def dsa_mixer(h, w_in, w_out, cos, sin):
    B, S, _ = h.shape
    topk = min(TOPK_MAX, S // 4)
    n_blocks = S // Q_BLOCK
    q, k, v, iq, ik, iw = split_cols(h @ w_in, IN1_COLS)
    q = apply_rope(q.reshape(B, S, ATT_HEADS, ATT_HEAD_DIM), cos, sin)
    k = apply_rope(k.reshape(B, S, 1, ATT_HEAD_DIM), cos, sin)[:, :, 0]
    iq = apply_rope(iq.reshape(B, S, IDX_HEADS, IDX_HEAD_DIM), cos, sin)
    ik = apply_rope(ik.reshape(B, S, 1, IDX_HEAD_DIM), cos, sin)[:, :, 0].astype(jnp.float32)
    iw = iw.astype(jnp.float32) * IDX_HEADS ** -0.5
    key_chunk = jnp.arange(S) // CHUNK

    def to_blocks(t):
        return jnp.swapaxes(t.reshape((B, n_blocks, Q_BLOCK) + t.shape[2:]), 0, 1)

    def block(args):
        qb, iqb, iwb, bi = args
        q_chunk = (bi * Q_BLOCK + jnp.arange(Q_BLOCK)) // CHUNK
        admissible = key_chunk[None, :] <= q_chunk[:, None]
        idx_logits = jnp.einsum('bqhd,bsd->bqhs', iqb.astype(jnp.float32), ik) * IDX_HEAD_DIM ** -0.5
        score = jnp.einsum('bqh,bqhs->bqs', iwb, jax.nn.relu(idx_logits))
        score = jnp.where(admissible[None], score, -jnp.inf)
        _, sel = lax.top_k(score, topk)
        valid = (sel // CHUNK) <= q_chunk[None, :, None]
        k_sel = jax.vmap(lambda kb, ib: kb[ib])(k, sel)
        v_sel = jax.vmap(lambda vb, ib: vb[ib])(v, sel)
        logits = jnp.einsum('bqhd,bqkd->bqhk', qb, k_sel).astype(jnp.float32) * ATT_HEAD_DIM ** -0.5
        logits = jnp.where(valid[:, :, None, :], logits, -jnp.inf)
        probs = jax.nn.softmax(logits, axis=-1).astype(v_sel.dtype)
        return jnp.einsum('bqhk,bqkd->bqhd', probs, v_sel)

    out = lax.map(block, (to_blocks(q), to_blocks(iq), to_blocks(iw), jnp.arange(n_blocks)))
    out = jnp.swapaxes(out, 0, 1).reshape(B, S, ATT_WIDTH)
    return out @ w_out


def setup_inputs(seed: int = 0) -> dict:
    key = jax.random.key(seed)
    ks = iter(jax.random.split(key, 32))
    nrm = lambda shape, scale: scale * jax.random.normal(next(ks), shape, jnp.float32)
    uni = lambda shape, lo, hi: jax.random.uniform(next(ks), shape, jnp.float32, lo, hi)
    D, E, O = D_MODEL, N_EVEN, N_ODD
    a_pow_c = uni((E, LRU_WIDTH), 0.9, 0.999)
    a_base = a_pow_c ** (1.0 / LRU_C)
    lru_lambda = jnp.log(a_base) - jnp.log1p(-a_base)
    return {
        'x': nrm((BATCH, SEQ, D), 1.0),
        'norm_mix': 1.0 + nrm((DEPTH, D), 0.02),
        'w_in0': nrm((E, D, IN0_WIDTH), D ** -0.5),
        'mu_shift': uni((E, RWKV_PROJ), 0.0, 1.0),
        'w_decay0': uni((E, RWKV_WIDTH), -6.0, -1.0),
        'w_decay_up': nrm((E, DECAY_LORA, RWKV_WIDTH), 0.1 * DECAY_LORA ** -0.5),
        'a_bias': nrm((E, RWKV_WIDTH), 0.5),
        'w_a_up': nrm((E, A_LORA, RWKV_WIDTH), 0.5 * A_LORA ** -0.5),
        'w_g_up': nrm((E, GATE_LORA, RWKV_WIDTH), GATE_LORA ** -0.5),
        'k_k': 0.85 + nrm((E, RWKV_WIDTH), 0.02),
        'k_a': 1.0 + nrm((E, RWKV_WIDTH), 0.02),
        'r_k': nrm((E, RWKV_HEADS, RWKV_HEAD_DIM), 0.1),
        'lnx_w': 1.0 + nrm((E, RWKV_WIDTH), 0.02),
        'lnx_b': nrm((E, RWKV_WIDTH), 0.02),
        'conv_w': nrm((E, CONV_WIDTH, LRU_WIDTH), CONV_WIDTH ** -0.5),
        'conv_b': nrm((E, LRU_WIDTH), 0.02),
        'w_rgate': nrm((E, LRU_HEADS, LRU_BLOCK, LRU_BLOCK), LRU_BLOCK ** -0.5),
        'b_rgate': nrm((E, LRU_HEADS, LRU_BLOCK), 0.02),
        'w_igate': nrm((E, LRU_HEADS, LRU_BLOCK, LRU_BLOCK), LRU_BLOCK ** -0.5),
        'b_igate': nrm((E, LRU_HEADS, LRU_BLOCK), 0.02),
        'lru_lambda': lru_lambda,
        'w_out0': nrm((E, MIX_WIDTH, D), MIX_WIDTH ** -0.5),
        'w_in1': nrm((O, D, IN1_WIDTH), D ** -0.5),
        'w_out1': nrm((O, ATT_WIDTH, D), ATT_WIDTH ** -0.5),
        'norm_ffn': 1.0 + nrm((DEPTH, D), 0.02),
        'ffn_gate': nrm((DEPTH, D, FFN_HIDDEN), D ** -0.5),
        'ffn_up': nrm((DEPTH, D, FFN_HIDDEN), D ** -0.5),
        'ffn_down': nrm((DEPTH, FFN_HIDDEN, D), FFN_HIDDEN ** -0.5),
        'norm_final': 1.0 + nrm((D,), 0.02),
    }


def reference(x, norm_mix, w_in0, mu_shift, w_decay0, w_decay_up, a_bias, w_a_up, w_g_up, k_k, k_a, r_k,
              lnx_w, lnx_b, conv_w, conv_b, w_rgate, b_rgate, w_igate, b_igate, lru_lambda, w_out0,
              w_in1, w_out1, norm_ffn, ffn_gate, ffn_up, ffn_down, norm_final):
    cos, sin = rope_tables(x.shape[1], ATT_HEAD_DIM)
    for layer in range(DEPTH):
        h = rmsnorm(x, norm_mix[layer])
        if layer % 2 == 0:
            e = layer // 2
            x = x + rwkv_lru_mixer(h, w_in0[e], mu_shift[e], w_decay0[e], w_decay_up[e], a_bias[e],
                                   w_a_up[e], w_g_up[e], k_k[e], k_a[e], r_k[e], lnx_w[e], lnx_b[e],
                                   conv_w[e], conv_b[e], w_rgate[e], b_rgate[e], w_igate[e], b_igate[e],
                                   lru_lambda[e], w_out0[e])
        else:
            o = layer // 2
            x = x + dsa_mixer(h, w_in1[o], w_out1[o], cos, sin)
        h = rmsnorm(x, norm_ffn[layer])
        x = x + swiglu(h, ffn_gate[layer], ffn_up[layer], ffn_down[layer])
    return rmsnorm(x, norm_final)
```

```python
import functools

import jax
import jax.numpy as jnp
from jax import lax
from jax.experimental import pallas as pl
from jax.experimental.pallas import tpu as pltpu

F32 = jnp.float32
BF16 = jnp.bfloat16
I32 = jnp.int32

NORM_EPS = 1e-6
GN_EPS = 64e-5
LRU_C = 8.0
ROPE_THETA = 10000.0
HEAD_DIM = 64
CHUNK = 64
Q_BLOCK = 128
TOPK_MAX = 256
CONV_WIDTH = 4
LANES = 128
SUBLANES = 8
VMEM_LIMIT_BYTES = 56 * 1024 * 1024
INT_MIN = -(2 ** 31)


def _cparams(*semantics):
    return pltpu.CompilerParams(dimension_semantics=semantics, vmem_limit_bytes=VMEM_LIMIT_BYTES)


def _rms(x, g):
    return x * lax.rsqrt(jnp.mean(x * x, axis=-1, keepdims=True) + NORM_EPS) * g


def _split_bf16(x, n):
    parts = []
    for _ in range(n):
        p = x.astype(BF16)
        parts.append(p)
        x = x - p.astype(F32)
    return parts


def _seg_sum(x, ones_bd):
    out = None
    for p in _split_bf16(x, 3):
        d = jnp.dot(p, ones_bd, preferred_element_type=F32)
        out = d if out is None else out + d
    return out


def _softplus(x):
    return jnp.maximum(x, 0.0) + jnp.log(1.0 + jnp.exp(-jnp.abs(x)))


def _gelu_tanh(x):
    cdf = 0.5 * (1.0 + jnp.tanh(0.7978845608028654 * (x + 0.044715 * (x * x * x))))
    return x * cdf


def _norm_matmul_kernel(x_ref, g_ref, w_ref, *o_refs, splits, n_chunk):
    h = _rms(x_ref[...], g_ref[...]).astype(BF16)
    off = 0
    for o_ref, n in zip(o_refs, splits):
        for c in range(0, n, n_chunk):
            cw = min(n_chunk, n - c)
            o_ref[:, c:c + cw] = jnp.dot(h, w_ref[:, off + c:off + c + cw], preferred_element_type=F32)
        off += n


def _norm_matmul(x, g, w, splits, tm):
    T, D = x.shape
    N = w.shape[1]
    assert sum(splits) == N and T % tm == 0
    return pl.pallas_call(
        functools.partial(_norm_matmul_kernel, splits=splits, n_chunk=512),
        grid=(T // tm,),
        in_specs=[pl.BlockSpec((tm, D), lambda i: (i, 0)),
                  pl.BlockSpec((1, D), lambda i: (0, 0)),
                  pl.BlockSpec((D, N), lambda i: (0, 0))],
        out_specs=[pl.BlockSpec((tm, n), lambda i: (i, 0)) for n in splits],
        out_shape=[jax.ShapeDtypeStruct((T, n), F32) for n in splits],
        compiler_params=_cparams("arbitrary"),
        name="norm_matmul",
    )(x, g.reshape(1, D), w)


def _rwkv_prep_kernel(p_ref, halo_ref, mu_ref, dec0_ref, wdu_ref, ab_ref, wau_ref, wgu_ref, kk_ref, ka_ref,
                      rk_ref, ones_ref, r_o, w_o, k_o, v_o, nkk_o, kka_o, bonus_o, g_o, *, tm, seq, width):
    i = pl.program_id(0)
    p = p_ref[...]
    at_seq_start = (i * tm) % seq == 0
    prev_row = jnp.where(at_seq_start, 0.0, halo_ref[SUBLANES - 1:SUBLANES, :])
    row = lax.broadcasted_iota(I32, p.shape, 0)
    prev = jnp.where(row == 0, prev_row, pltpu.roll(p, 1, axis=0))
    ps = p + (prev - p) * mu_ref[...]
    W = width
    r, k, v = ps[:, 0:W], ps[:, W:2 * W], ps[:, 2 * W:3 * W]
    lora = ps[:, 3 * W:3 * W + LANES]
    gd = ps[:, 3 * W + LANES:3 * W + 2 * LANES]
    ones_bd = ones_ref[...]

    z = dec0_ref[...] + jnp.dot(jnp.tanh(lora).astype(BF16), wdu_ref[...], preferred_element_type=F32)
    w_log = -_softplus(-z) - 0.5
    w_o[...] = jnp.exp(-jnp.exp(w_log))
    a = jax.nn.sigmoid(ab_ref[...] + jnp.dot(lora.astype(BF16), wau_ref[...], preferred_element_type=F32))
    g_o[...] = jnp.dot(jax.nn.sigmoid(gd).astype(BF16), wgu_ref[...], preferred_element_type=F32)
    kk = k * kk_ref[...]
    kk = kk * lax.rsqrt(jnp.maximum(_seg_sum(kk * kk, ones_bd), 1e-24))
    k_mod = k * (1.0 + (a - 1.0) * ka_ref[...])
    r_o[...] = r
    k_o[...] = k_mod
    v_o[...] = v
    nkk_o[...] = -kk
    kka_o[...] = kk * a
    bonus_o[...] = _seg_sum(r * k_mod * rk_ref[...], ones_bd) * v


def _rwkv_prep(rwkv_p, mu, dec0, wdu_pad, a_bias, wau_pad, wgu, k_k, k_a, r_k, ones_bd, seq, tm):
    T, P = rwkv_p.shape
    W = dec0.shape[-1]
    assert T % tm == 0 and seq % tm == 0 and tm % SUBLANES == 0
    row = lambda a: a.reshape(1, -1)
    full = lambda a: pl.BlockSpec(a.shape, lambda i: (0,) * a.ndim)
    params = [row(mu), row(dec0), wdu_pad, row(a_bias), wau_pad, wgu, row(k_k), row(k_a), row(r_k), ones_bd]
    return pl.pallas_call(
        functools.partial(_rwkv_prep_kernel, tm=tm, seq=seq, width=W),
        grid=(T // tm,),
        in_specs=[pl.BlockSpec((tm, P), lambda i: (i, 0)),
                  pl.BlockSpec((SUBLANES, P), lambda i: (jnp.maximum(i * (tm // SUBLANES) - 1, 0), 0))]
                 + [full(a) for a in params],
        out_specs=[pl.BlockSpec((tm, W), lambda i: (i, 0))] * 8,
        out_shape=[jax.ShapeDtypeStruct((T, W), F32)] * 8,
        compiler_params=_cparams("arbitrary"),
        name="rwkv_prep",
    )(rwkv_p, rwkv_p, *params)


def _rwkv_scan_kernel(r_ref, w_ref, k_ref, v_ref, nkk_ref, kka_ref, y_ref, s_ref, *, steps, n):
    @pl.when(pl.program_id(0) == 0)
    def _():
        s_ref[...] = jnp.zeros_like(s_ref)

    def step(t, carry):
        v_t = v_ref[t]
        sa = s_ref[0] * nkk_ref[t, 0:1, :]
        for kidx in range(1, n):
            sa = sa + s_ref[kidx] * nkk_ref[t, kidx:kidx + 1, :]
        y = None
        for kidx in range(n):
            s_new = (s_ref[kidx] * w_ref[t, kidx:kidx + 1, :] + sa * kka_ref[t, kidx:kidx + 1, :]
                     + v_t * k_ref[t, kidx:kidx + 1, :])
            s_ref[kidx] = s_new
            yk = s_new * r_ref[t, kidx:kidx + 1, :]
            y = yk if y is None else y + yk
        y_ref[t] = y
        return carry

    lax.fori_loop(0, steps, step, 0)


def _rwkv_scan(r, w, k, v, nkk, kka, steps):
    S, N, L = r.shape
    assert S % steps == 0
    spec = pl.BlockSpec((steps, N, L), lambda i: (i, 0, 0))
    return pl.pallas_call(
        functools.partial(_rwkv_scan_kernel, steps=steps, n=N),
        grid=(S // steps,),
        in_specs=[spec] * 6,
        out_specs=spec,
        out_shape=jax.ShapeDtypeStruct((S, N, L), F32),
        scratch_shapes=[pltpu.VMEM((N, N, L), F32)],
        compiler_params=_cparams("arbitrary"),
        name="rwkv_scan",
    )(r, w, k, v, nkk, kka)


def _lru_kernel(x_ref, halo_ref, gate_ref, cw_ref, cb_ref, wr_ref, br_ref, wi_ref, bi_ref, lam_ref, o_ref,
                h_scr, *, ts):
    j = pl.program_id(1)

    @pl.when(j == 0)
    def _():
        h_scr[...] = jnp.zeros_like(h_scr)

    x = x_ref[...]
    halo = jnp.where(j == 0, 0.0, halo_ref[...])
    row8 = lax.broadcasted_iota(I32, halo.shape, 0)
    xc = cb_ref[...] + x * cw_ref[CONV_WIDTH - 1:CONV_WIDTH, :]
    for d in range(1, CONV_WIDTH):
        rolled = pltpu.roll(x, d, axis=0)
        head = jnp.where(row8 < d, pltpu.roll(halo, d, axis=0), rolled[0:SUBLANES])
        shifted = jnp.concatenate([head, rolled[SUBLANES:]], axis=0)
        xc = xc + shifted * cw_ref[CONV_WIDTH - 1 - d:CONV_WIDTH - d, :]

    xcb = xc.astype(BF16)
    rg = jax.nn.sigmoid(jnp.dot(xcb, wr_ref[...], preferred_element_type=F32) + br_ref[...])
    ig = jax.nn.sigmoid(jnp.dot(xcb, wi_ref[...], preferred_element_type=F32) + bi_ref[...])
    log_a = -LRU_C * rg * _softplus(-lam_ref[...])
    a_cum = jnp.exp(log_a)
    th = jnp.tanh(log_a)
    x_cum = xc * ig * jnp.sqrt(-2.0 * th / (1.0 - th))

    row = lax.broadcasted_iota(I32, x.shape, 0)
    d = 1
    while d < ts:
        keep = row >= d
        a_prev = jnp.where(keep, pltpu.roll(a_cum, d, axis=0), 1.0)
        x_prev = jnp.where(keep, pltpu.roll(x_cum, d, axis=0), 0.0)
        x_cum = a_cum * x_prev + x_cum
        a_cum = a_cum * a_prev
        d *= 2
    h = x_cum + a_cum * h_scr[...]
    h_scr[...] = h[ts - 1:ts, :]
    o_ref[...] = h * _gelu_tanh(gate_ref[...])


def _lru(lru_x, lru_gate, conv_w, conv_b, wr_bd, b_r, wi_bd, b_i, lam, batch, seq, ts):
    T, W = lru_x.shape
    assert seq % ts == 0 and ts % SUBLANES == 0
    nt = seq // ts
    row = lambda a: a.reshape(1, -1)
    full = lambda a: pl.BlockSpec(a.shape, lambda b, j: (0,) * a.ndim)
    params = [conv_w, row(conv_b), wr_bd, row(b_r), wi_bd, row(b_i), row(lam)]
    return pl.pallas_call(
        functools.partial(_lru_kernel, ts=ts),
        grid=(batch, nt),
        in_specs=[pl.BlockSpec((ts, W), lambda b, j: (b * nt + j, 0)),
                  pl.BlockSpec((SUBLANES, W),
                               lambda b, j: (jnp.maximum((b * nt + j) * (ts // SUBLANES) - 1, 0), 0)),
                  pl.BlockSpec((ts, W), lambda b, j: (b * nt + j, 0))]
                 + [full(a) for a in params],
        out_specs=pl.BlockSpec((ts, W), lambda b, j: (b * nt + j, 0)),
        out_shape=jax.ShapeDtypeStruct((T, W), F32),
        scratch_shapes=[pltpu.VMEM((1, W), F32)],
        compiler_params=_cparams("arbitrary", "arbitrary"),
        name="rg_lru",
    )(lru_x, lru_x, lru_gate, *params)


def _mix_out_kernel(y_ref, bonus_ref, g_ref, lru_ref, x_ref, lw_ref, lb_ref, ones_ref, wo_ref, o_ref, *, width):
    ones_bd = ones_ref[...]
    y = y_ref[...]
    yc = y - _seg_sum(y, ones_bd) * (1.0 / HEAD_DIM)
    var = _seg_sum(yc * yc, ones_bd) * (1.0 / HEAD_DIM)
    yn = yc * lax.rsqrt(var + GN_EPS) * lw_ref[...] + lb_ref[...]
    rwkv_out = ((yn + bonus_ref[...]) * g_ref[...]).astype(BF16)
    out = x_ref[...] + jnp.dot(rwkv_out, wo_ref[0:width, :], preferred_element_type=F32)
    o_ref[...] = out + jnp.dot(lru_ref[...].astype(BF16), wo_ref[width:, :], preferred_element_type=F32)


def _mix_out(y, bonus, g, lru_out, x, lnx_w, lnx_b, ones_bd, w_out, tm):
    T, W = y.shape
    D = x.shape[1]
    row = lambda a: a.reshape(1, -1)
    full = lambda a: pl.BlockSpec(a.shape, lambda i: (0,) * a.ndim)
    params = [row(lnx_w), row(lnx_b), ones_bd, w_out]
    act = pl.BlockSpec((tm, W), lambda i: (i, 0))
    return pl.pallas_call(
        functools.partial(_mix_out_kernel, width=W),
        grid=(T // tm,),
        in_specs=[act, act, act, act, pl.BlockSpec((tm, D), lambda i: (i, 0))] + [full(a) for a in params],
        out_specs=pl.BlockSpec((tm, D), lambda i: (i, 0)),
        out_shape=jax.ShapeDtypeStruct((T, D), F32),
        compiler_params=_cparams("arbitrary"),
        name="mix_out",
    )(y, bonus, g, lru_out, x, *params)


def _matmul_res_kernel(a_ref, w_ref, x_ref, o_ref):
    o_ref[...] = x_ref[...] + jnp.dot(a_ref[...].astype(BF16), w_ref[...], preferred_element_type=F32)


def _matmul_res(a, w, x, tm):
    T, K = a.shape
    D = w.shape[1]
    return pl.pallas_call(
        _matmul_res_kernel,
        grid=(T // tm,),
        in_specs=[pl.BlockSpec((tm, K), lambda i: (i, 0)),
                  pl.BlockSpec((K, D), lambda i: (0, 0)),
                  pl.BlockSpec((tm, D), lambda i: (i, 0))],
        out_specs=pl.BlockSpec((tm, D), lambda i: (i, 0)),
        out_shape=jax.ShapeDtypeStruct((T, D), F32),
        compiler_params=_cparams("arbitrary"),
        name="matmul_res",
    )(a, w, x)


def _ffn_kernel(x_ref, g_ref, wg_ref, wu_ref, wd_ref, gf_ref, o_ref, *, th, final_norm):
    x = x_ref[...]
    h = _rms(x, g_ref[...]).astype(BF16)
    acc = x
    hidden = wg_ref.shape[1]
    for c in range(0, hidden, th):
        gt = jnp.dot(h, wg_ref[:, c:c + th], preferred_element_type=F32)
        ut = jnp.dot(h, wu_ref[:, c:c + th], preferred_element_type=F32)
        act = (gt * jax.nn.sigmoid(gt) * ut).astype(BF16)
        acc = acc + jnp.dot(act, wd_ref[c:c + th, :], preferred_element_type=F32)
    if final_norm:
        acc = _rms(acc, gf_ref[...])
    o_ref[...] = acc


def _ffn(x, g, wg, wu, wd, g_final, final_norm, tm, th):
    T, D = x.shape
    H = wg.shape[1]
    assert H % th == 0 and T % tm == 0
    resident = lambda a: pl.BlockSpec(a.shape, lambda i: (0,) * a.ndim, pipeline_mode=pl.Buffered(1))
    return pl.pallas_call(
        functools.partial(_ffn_kernel, th=th, final_norm=final_norm),
        grid=(T // tm,),
        in_specs=[pl.BlockSpec((tm, D), lambda i: (i, 0)),
                  pl.BlockSpec((1, D), lambda i: (0, 0)),
                  resident(wg), resident(wu), resident(wd),
                  pl.BlockSpec((1, D), lambda i: (0, 0))],
        out_specs=pl.BlockSpec((tm, D), lambda i: (i, 0)),
        out_shape=jax.ShapeDtypeStruct((T, D), F32),
        compiler_params=_cparams("arbitrary"),
        name="ffn",
    )(x, g.reshape(1, D), wg, wu, wd, g_final.reshape(1, D))


def _rope(x, cos2, sin2):
    lane = lax.broadcasted_iota(I32, x.shape, 1)
    first_half = (lane % HEAD_DIM) < HEAD_DIM // 2
    partner = jnp.where(first_half, pltpu.roll(x, LANES - HEAD_DIM // 2, axis=1),
                        pltpu.roll(x, HEAD_DIM // 2, axis=1))
    return x * cos2 + partner * sin2


def _dsa_kernel(q_ref, iq_ref, qm_ref, kvm_ref, cosk_ref, sink_ref, cosq_ref, sinq_ref, o_ref,
                kT_scr, ikT_scr, v_scr, qs_scr, iqs_scr, key_scr, oh_scr,
                *, sk, q0, topk, heads, iheads, kt, hg):
    j = pl.program_id(1)
    QB, HD = Q_BLOCK, HEAD_DIM
    scale = HD ** -0.5

    @pl.when(j == 0)
    def _():
        kv = kvm_ref[...]
        kr_t = _rope(kv[:, 0:LANES], cosk_ref[...], sink_ref[...]).T
        kT_scr[...] = kr_t[0:HD].astype(BF16)
        b1, b2 = _split_bf16(kr_t[HD:2 * HD], 2)
        ikT_scr[0 * HD:1 * HD] = b1
        ikT_scr[1 * HD:2 * HD] = b1
        ikT_scr[2 * HD:3 * HD] = b2
        ikT_scr[3 * HD:4 * HD] = b2
        v_scr[...] = kv[:, LANES:LANES + HD].astype(BF16)

    cq, sq = cosq_ref[...], sinq_ref[...]
    q = q_ref[...]
    for c in range(heads // 2):
        ch = (_rope(q[:, c * LANES:(c + 1) * LANES], cq, sq) * scale).astype(BF16)
        qs_scr[(2 * c) * QB:(2 * c + 1) * QB, :] = ch[:, 0:HD]
        qs_scr[(2 * c + 1) * QB:(2 * c + 2) * QB, :] = ch[:, HD:2 * HD]
    iq = iq_ref[...]
    for c in range(iheads // 2):
        a1, a2 = _split_bf16(_rope(iq[:, c * LANES:(c + 1) * LANES], cq, sq) * scale, 2)
        for hh in range(2):
            rows = slice((2 * c + hh) * QB, (2 * c + hh + 1) * QB)
            a1h, a2h = a1[:, hh * HD:(hh + 1) * HD], a2[:, hh * HD:(hh + 1) * HD]
            iqs_scr[rows, 0 * HD:1 * HD] = a1h
            iqs_scr[rows, 1 * HD:2 * HD] = a2h
            iqs_scr[rows, 2 * HD:3 * HD] = a1h
            iqs_scr[rows, 3 * HD:4 * HD] = a2h

    iw = qm_ref[:, LANES + HD:LANES + HD + iheads] * (iheads ** -0.5)
    q_chunk = (q0 + j * QB + lax.broadcasted_iota(I32, (QB, kt), 0)) // CHUNK
    for t in range(sk // kt):
        logit = jnp.dot(iqs_scr[...], ikT_scr[:, t * kt:(t + 1) * kt], preferred_element_type=F32)
        score = None
        for h in range(iheads):
            term = iw[:, h:h + 1] * jnp.maximum(logit[h * QB:(h + 1) * QB], 0.0)
            score = term if score is None else score + term
        k_chunk = (t * kt + lax.broadcasted_iota(I32, (QB, kt), 1)) // CHUNK
        bits = lax.bitcast_convert_type(score, I32)
        key = jnp.where(bits < 0, bits ^ 0x7FFFFFFF, bits)
        key_scr[:, t * kt:(t + 1) * kt] = jnp.where(k_chunk <= q_chunk, key, INT_MIN)

    def bit_step(i, thr):
        cand = thr ^ lax.shift_left(jnp.int32(1), 31 - i)
        cnt = jnp.sum((key_scr[...] >= cand).astype(I32), axis=1, keepdims=True)
        return jnp.where(cnt >= topk, cand, thr)

    thr = lax.fori_loop(0, 32, bit_step, jnp.full((QB, 1), INT_MIN, I32))
    thr = jnp.maximum(thr, INT_MIN + 1)
    bias = jnp.where(key_scr[...] >= thr, 0.0, -jnp.inf)

    def head_group(gi, carry):
        r0 = pl.multiple_of(gi * (hg * QB), hg * QB)
        lg = jnp.dot(qs_scr[pl.ds(r0, hg * QB), :], kT_scr[...], preferred_element_type=F32)
        lg = lg.reshape(hg, QB, sk) + bias[None]
        p = jnp.exp(lg - jnp.max(lg, axis=-1, keepdims=True))
        denom = jnp.sum(p, axis=-1, keepdims=True).reshape(hg * QB, 1)
        o = jnp.dot(p.reshape(hg * QB, sk).astype(BF16), v_scr[...], preferred_element_type=F32)
        oh_scr[pl.ds(r0, hg * QB), :] = o / denom
        return carry

    lax.fori_loop(0, heads // hg, head_group, 0)
    for h in range(heads):
        o_ref[:, h * HD:(h + 1) * HD] = oh_scr[h * QB:(h + 1) * QB, :]


def _dsa_group(q, iq, kvm, cos2, sin2, batch, seq, g, gq, topk, heads, iheads):
    QB, HD = Q_BLOCK, HEAD_DIM
    nq = seq // QB
    sk = (g + 1) * gq * QB
    kt = min(sk, 512)
    hg = 4
    misc = kvm.shape[1]
    kvm3 = kvm.reshape(batch, seq, misc)
    qrow = lambda b, j: (b * nq + g * gq + j, 0)
    kernel = functools.partial(_dsa_kernel, sk=sk, q0=g * gq * QB, topk=topk, heads=heads, iheads=iheads,
                               kt=kt, hg=hg)
    return pl.pallas_call(
        kernel,
        grid=(batch, gq),
        in_specs=[pl.BlockSpec((QB, heads * HD), qrow),
                  pl.BlockSpec((QB, iheads * HD), qrow),
                  pl.BlockSpec((QB, misc), qrow),
                  pl.BlockSpec((None, sk, misc), lambda b, j: (b, 0, 0)),
                  pl.BlockSpec((sk, LANES), lambda b, j: (0, 0)),
                  pl.BlockSpec((sk, LANES), lambda b, j: (0, 0)),
                  pl.BlockSpec((QB, LANES), lambda b, j: (g * gq + j, 0)),
                  pl.BlockSpec((QB, LANES), lambda b, j: (g * gq + j, 0))],
        out_specs=pl.BlockSpec((QB, heads * HD), lambda b, j: (b * gq + j, 0)),
        out_shape=jax.ShapeDtypeStruct((batch * gq * QB, heads * HD), F32),
        scratch_shapes=[pltpu.VMEM((HD, sk), BF16),
                        pltpu.VMEM((4 * HD, sk), BF16),
                        pltpu.VMEM((sk, HD), BF16),
                        pltpu.VMEM((heads * QB, HD), BF16),
                        pltpu.VMEM((iheads * QB, 4 * HD), BF16),
                        pltpu.VMEM((QB, sk), I32),
                        pltpu.VMEM((heads * QB, HD), F32)],
        compiler_params=_cparams("arbitrary", "arbitrary"),
        name=f"dsa_g{g}",
    )(q, iq, kvm, kvm3, cos2, sin2, cos2, sin2)


def _rope_tables(seq):
    half = HEAD_DIM // 2
    inv_freq = ROPE_THETA ** (-jnp.arange(0, HEAD_DIM, 2, dtype=F32) / HEAD_DIM)
    ang = jnp.arange(seq, dtype=F32)[:, None] * inv_freq[None, :]
    cos, sin = jnp.cos(ang), jnp.sin(ang)
    cos2 = jnp.tile(cos, (1, LANES // half))
    sin2 = jnp.tile(jnp.concatenate([-sin, sin], axis=1), (1, LANES // HEAD_DIM))
    return cos2, sin2


def _block_diag(w):
    h, n, _ = w.shape
    eye = jnp.eye(h, dtype=w.dtype)
    return (eye[:, None, :, None] * w[:, :, None, :]).reshape(h * n, h * n)


def kernel(x, norm_mix, w_in0, mu_shift, w_decay0, w_decay_up, a_bias, w_a_up, w_g_up, k_k, k_a, r_k, lnx_w, lnx_b, conv_w, conv_b, w_rgate, b_rgate, w_igate, b_igate, lru_lambda, w_out0, w_in1, w_out1, norm_ffn, ffn_gate, ffn_up, ffn_down, norm_final):
    B, S, D = x.shape
    T = B * S
    HD = HEAD_DIM
    W = w_decay0.shape[-1]
    H = W // HD
    dlora, alora = w_decay_up.shape[1], w_a_up.shape[1]
    assert dlora + alora == LANES and w_g_up.shape[1] == LANES
    x2 = x.reshape(T, D)
    tm = min(512, S)

    rwkv_cols = 3 * W + 2 * LANES
    rwkv_p, lru_x, lru_gate = _norm_matmul(x2, norm_mix[0], w_in0[0].astype(BF16), (rwkv_cols, W, W), tm)
    ones_bd = _block_diag(jnp.ones((H, HD, HD), BF16))
    wdu_pad = jnp.concatenate([w_decay_up[0], jnp.zeros((alora, W), F32)], axis=0).astype(BF16)
    wau_pad = jnp.concatenate([jnp.zeros((dlora, W), F32), w_a_up[0]], axis=0).astype(BF16)
    r, w, k, v, nkk, kka, bonus, g = _rwkv_prep(
        rwkv_p, mu_shift[0], w_decay0[0], wdu_pad, a_bias[0], wau_pad, w_g_up[0].astype(BF16), k_k[0], k_a[0],
        r_k[0].reshape(-1), ones_bd, S, min(256, S))
    to_scan = lambda t: t.reshape(B, S, H, HD).transpose(1, 3, 0, 2).reshape(S, HD, B * H)
    y = _rwkv_scan(*(to_scan(t) for t in (r, w, k, v, nkk, kka)), steps=16)
    y = y.reshape(S, HD, B, H).transpose(2, 0, 3, 1).reshape(T, W)
    lru_out = _lru(lru_x, lru_gate, conv_w[0], conv_b[0], _block_diag(w_rgate[0]).astype(BF16),
                   b_rgate[0].reshape(-1), _block_diag(w_igate[0]).astype(BF16), b_igate[0].reshape(-1),
                   lru_lambda[0], B, S, min(256, S))
    x2 = _mix_out(y, bonus, g, lru_out, x2, lnx_w[0], lnx_b[0], ones_bd, w_out0[0].astype(BF16), tm)
    x2 = _ffn(x2, norm_ffn[0], ffn_gate[0].astype(BF16), ffn_up[0].astype(BF16), ffn_down[0].astype(BF16),
              norm_final, False, tm, 256)

    heads = w_out1.shape[1] // HD
    iheads = (w_in1.shape[2] - heads * HD - 3 * HD) // (HD + 1)
    c0 = heads * HD
    wq, wk, wv = w_in1[0][:, :c0], w_in1[0][:, c0:c0 + HD], w_in1[0][:, c0 + HD:c0 + 2 * HD]
    wiq = w_in1[0][:, c0 + 2 * HD:c0 + 2 * HD + iheads * HD]
    wik = w_in1[0][:, c0 + 2 * HD + iheads * HD:c0 + 3 * HD + iheads * HD]
    wiw = w_in1[0][:, c0 + 3 * HD + iheads * HD:]
    misc = 2 * LANES
    pad = jnp.zeros((D, misc - 3 * HD - iheads), F32)
    w1 = jnp.concatenate([wq, wiq, wk, wik, wv, wiw, pad], axis=1).astype(BF16)
    q, iq, kvm = _norm_matmul(x2, norm_mix[1], w1, (heads * HD, iheads * HD, misc), tm)
    cos2, sin2 = _rope_tables(S)
    topk = min(TOPK_MAX, S // 4)
    nq = S // Q_BLOCK
    gq = 4 if nq % 4 == 0 and nq > 4 else 1
    outs = [_dsa_group(q, iq, kvm, cos2, sin2, B, S, gi, gq, topk, heads, iheads).reshape(B, gq * Q_BLOCK, c0)
            for gi in range(nq // gq)]
    attn = jnp.concatenate(outs, axis=1).reshape(T, c0)
    x2 = _matmul_res(attn, w_out1[0].astype(BF16), x2, tm)
    x2 = _ffn(x2, norm_ffn[1], ffn_gate[1].astype(BF16), ffn_up[1].astype(BF16), ffn_down[1].astype(BF16),
              norm_final, True, tm, 256)
    return x2.reshape(B, S, D)
```

```python
import functools

import jax
import jax.numpy as jnp
from jax import lax
from jax.experimental import pallas as pl
from jax.experimental.pallas import tpu as pltpu

F32 = jnp.float32
BF16 = jnp.bfloat16
I32 = jnp.int32

NORM_EPS = 1e-6
GN_EPS = 64e-5
LRU_C = 8.0
ROPE_THETA = 10000.0
HEAD_DIM = 64
CHUNK = 64
Q_BLOCK = 128
TOPK_MAX = 256
CONV_WIDTH = 4
LANES = 128
SUBLANES = 8
VMEM_LIMIT_BYTES = 56 * 1024 * 1024
INT_MIN = -(2 ** 31)


def _cparams(*semantics):
    return pltpu.CompilerParams(dimension_semantics=semantics, vmem_limit_bytes=VMEM_LIMIT_BYTES)


def _rms(x, g):
    return x * lax.rsqrt(jnp.mean(x * x, axis=-1, keepdims=True) + NORM_EPS) * g


def _split_bf16(x, n):
    parts = []
    for _ in range(n):
        p = x.astype(BF16)
        parts.append(p)
        x = x - p.astype(F32)
    return parts


def _seg_sum(x, ones_bd):
    out = None
    for p in _split_bf16(x, 3):
        d = jnp.dot(p, ones_bd, preferred_element_type=F32)
        out = d if out is None else out + d
    return out


def _softplus(x):
    return jnp.maximum(x, 0.0) + jnp.log(1.0 + jnp.exp(-jnp.abs(x)))


def _gelu_tanh(x):
    cdf = 0.5 * (1.0 + jnp.tanh(0.7978845608028654 * (x + 0.044715 * (x * x * x))))
    return x * cdf


def _norm_matmul_kernel(x_ref, g_ref, w_ref, *o_refs, splits, n_chunk):
    h = _rms(x_ref[...], g_ref[...]).astype(BF16)
    off = 0
    for o_ref, n in zip(o_refs, splits):
        for c in range(0, n, n_chunk):
            cw = min(n_chunk, n - c)
            o_ref[:, c:c + cw] = jnp.dot(h, w_ref[:, off + c:off + c + cw], preferred_element_type=F32)
        off += n


def _norm_matmul(x, g, w, splits, tm):
    T, D = x.shape
    N = w.shape[1]
    assert sum(splits) == N and T % tm == 0
    return pl.pallas_call(
        functools.partial(_norm_matmul_kernel, splits=splits, n_chunk=512),
        grid=(T // tm,),
        in_specs=[pl.BlockSpec((tm, D), lambda i: (i, 0)),
                  pl.BlockSpec((1, D), lambda i: (0, 0)),
                  pl.BlockSpec((D, N), lambda i: (0, 0))],
        out_specs=[pl.BlockSpec((tm, n), lambda i: (i, 0)) for n in splits],
        out_shape=[jax.ShapeDtypeStruct((T, n), F32) for n in splits],
        compiler_params=_cparams("arbitrary"),
        name="norm_matmul",
    )(x, g.reshape(1, D), w)


def _rwkv_prep_kernel(p_ref, halo_ref, mu_ref, dec0_ref, wdu_ref, ab_ref, wau_ref, wgu_ref, kk_ref, ka_ref,
                      rk_ref, ones_ref, r_o, w_o, k_o, v_o, nkk_o, kka_o, bonus_o, g_o, *, tm, seq, width):
    i = pl.program_id(0)
    p = p_ref[...]
    at_seq_start = (i * tm) % seq == 0
    prev_row = jnp.where(at_seq_start, 0.0, halo_ref[SUBLANES - 1:SUBLANES, :])
    row = lax.broadcasted_iota(I32, p.shape, 0)
    prev = jnp.where(row == 0, prev_row, pltpu.roll(p, 1, axis=0))
    ps = p + (prev - p) * mu_ref[...]
    W = width
    r, k, v = ps[:, 0:W], ps[:, W:2 * W], ps[:, 2 * W:3 * W]
    lora = ps[:, 3 * W:3 * W + LANES]
    gd = ps[:, 3 * W + LANES:3 * W + 2 * LANES]
    ones_bd = ones_ref[...]

    z = dec0_ref[...] + jnp.dot(jnp.tanh(lora).astype(BF16), wdu_ref[...], preferred_element_type=F32)
    w_log = -_softplus(-z) - 0.5
    w_o[...] = jnp.exp(-jnp.exp(w_log))
    a = jax.nn.sigmoid(ab_ref[...] + jnp.dot(lora.astype(BF16), wau_ref[...], preferred_element_type=F32))
    g_o[...] = jnp.dot(jax.nn.sigmoid(gd).astype(BF16), wgu_ref[...], preferred_element_type=F32)
    kk = k * kk_ref[...]
    kk = kk * lax.rsqrt(jnp.maximum(_seg_sum(kk * kk, ones_bd), 1e-24))
    k_mod = k * (1.0 + (a - 1.0) * ka_ref[...])
    r_o[...] = r
    k_o[...] = k_mod
    v_o[...] = v
    nkk_o[...] = -kk
    kka_o[...] = kk * a
    bonus_o[...] = _seg_sum(r * k_mod * rk_ref[...], ones_bd) * v


def _rwkv_prep(rwkv_p, mu, dec0, wdu_pad, a_bias, wau_pad, wgu, k_k, k_a, r_k, ones_bd, seq, tm):
    T, P = rwkv_p.shape
    W = dec0.shape[-1]
    assert T % tm == 0 and seq % tm == 0 and tm % SUBLANES == 0
    row = lambda a: a.reshape(1, -1)
    full = lambda a: pl.BlockSpec(a.shape, lambda i: (0,) * a.ndim)
    params = [row(mu), row(dec0), wdu_pad, row(a_bias), wau_pad, wgu, row(k_k), row(k_a), row(r_k), ones_bd]
    return pl.pallas_call(
        functools.partial(_rwkv_prep_kernel, tm=tm, seq=seq, width=W),
        grid=(T // tm,),
        in_specs=[pl.BlockSpec((tm, P), lambda i: (i, 0)),
                  pl.BlockSpec((SUBLANES, P), lambda i: (jnp.maximum(i * (tm // SUBLANES) - 1, 0), 0))]
                 + [full(a) for a in params],
        out_specs=[pl.BlockSpec((tm, W), lambda i: (i, 0))] * 8,
        out_shape=[jax.ShapeDtypeStruct((T, W), F32)] * 8,
        compiler_params=_cparams("arbitrary"),
        name="rwkv_prep",
    )(rwkv_p, rwkv_p, *params)


def _rwkv_scan_kernel(r_ref, w_ref, k_ref, v_ref, nkk_ref, kka_ref, y_ref, s_ref, *, steps, n):
    @pl.when(pl.program_id(0) == 0)
    def _():
        s_ref[...] = jnp.zeros_like(s_ref)

    def step(t, carry):
        v_t = v_ref[t]
        sa = s_ref[0] * nkk_ref[t, 0:1, :]
        for kidx in range(1, n):
            sa = sa + s_ref[kidx] * nkk_ref[t, kidx:kidx + 1, :]
        y = None
        for kidx in range(n):
            s_new = (s_ref[kidx] * w_ref[t, kidx:kidx + 1, :] + sa * kka_ref[t, kidx:kidx + 1, :]
                     + v_t * k_ref[t, kidx:kidx + 1, :])
            s_ref[kidx] = s_new
            yk = s_new * r_ref[t, kidx:kidx + 1, :]
            y = yk if y is None else y + yk
        y_ref[t] = y
        return carry

    lax.fori_loop(0, steps, step, 0)


def _rwkv_scan(r, w, k, v, nkk, kka, steps):
    S, N, L = r.shape
    assert S % steps == 0
    spec = pl.BlockSpec((steps, N, L), lambda i: (i, 0, 0))
    return pl.pallas_call(
        functools.partial(_rwkv_scan_kernel, steps=steps, n=N),
        grid=(S // steps,),
        in_specs=[spec] * 6,
        out_specs=spec,
        out_shape=jax.ShapeDtypeStruct((S, N, L), F32),
        scratch_shapes=[pltpu.VMEM((N, N, L), F32)],
        compiler_params=_cparams("arbitrary"),
        name="rwkv_scan",
    )(r, w, k, v, nkk, kka)


def _lru_kernel(x_ref, halo_ref, gate_ref, cw_ref, cb_ref, wr_ref, br_ref, wi_ref, bi_ref, lam_ref, o_ref,
                h_scr, *, ts):
    j = pl.program_id(1)

    @pl.when(j == 0)
    def _():
        h_scr[...] = jnp.zeros_like(h_scr)

    x = x_ref[...]
    halo = jnp.where(j == 0, 0.0, halo_ref[...])
    row8 = lax.broadcasted_iota(I32, halo.shape, 0)
    xc = cb_ref[...] + x * cw_ref[CONV_WIDTH - 1:CONV_WIDTH, :]
    for d in range(1, CONV_WIDTH):
        rolled = pltpu.roll(x, d, axis=0)
        head = jnp.where(row8 < d, pltpu.roll(halo, d, axis=0), rolled[0:SUBLANES])
        shifted = jnp.concatenate([head, rolled[SUBLANES:]], axis=0)
        xc = xc + shifted * cw_ref[CONV_WIDTH - 1 - d:CONV_WIDTH - d, :]

    xcb = xc.astype(BF16)
    rg = jax.nn.sigmoid(jnp.dot(xcb, wr_ref[...], preferred_element_type=F32) + br_ref[...])
    ig = jax.nn.sigmoid(jnp.dot(xcb, wi_ref[...], preferred_element_type=F32) + bi_ref[...])
    log_a = -LRU_C * rg * _softplus(-lam_ref[...])
    a_cum = jnp.exp(log_a)
    th = jnp.tanh(log_a)
    x_cum = xc * ig * jnp.sqrt(-2.0 * th / (1.0 - th))

    row = lax.broadcasted_iota(I32, x.shape, 0)
    d = 1
    while d < ts:
        keep = row >= d
        a_prev = jnp.where(keep, pltpu.roll(a_cum, d, axis=0), 1.0)
        x_prev = jnp.where(keep, pltpu.roll(x_cum, d, axis=0), 0.0)
        x_cum = a_cum * x_prev + x_cum
        a_cum = a_cum * a_prev
        d *= 2
    h = x_cum + a_cum * h_scr[...]
    h_scr[...] = h[ts - 1:ts, :]
    o_ref[...] = h * _gelu_tanh(gate_ref[...])


def _lru(lru_x, lru_gate, conv_w, conv_b, wr_bd, b_r, wi_bd, b_i, lam, batch, seq, ts):
    T, W = lru_x.shape
    assert seq % ts == 0 and ts % SUBLANES == 0
    nt = seq // ts
    row = lambda a: a.reshape(1, -1)
    full = lambda a: pl.BlockSpec(a.shape, lambda b, j: (0,) * a.ndim)
    params = [conv_w, row(conv_b), wr_bd, row(b_r), wi_bd, row(b_i), row(lam)]
    return pl.pallas_call(
        functools.partial(_lru_kernel, ts=ts),
        grid=(batch, nt),
        in_specs=[pl.BlockSpec((ts, W), lambda b, j: (b * nt + j, 0)),
                  pl.BlockSpec((SUBLANES, W),
                               lambda b, j: (jnp.maximum((b * nt + j) * (ts // SUBLANES) - 1, 0), 0)),
                  pl.BlockSpec((ts, W), lambda b, j: (b * nt + j, 0))]
                 + [full(a) for a in params],
        out_specs=pl.BlockSpec((ts, W), lambda b, j: (b * nt + j, 0)),
        out_shape=jax.ShapeDtypeStruct((T, W), F32),
        scratch_shapes=[pltpu.VMEM((1, W), F32)],
        compiler_params=_cparams("arbitrary", "arbitrary"),
        name="rg_lru",
    )(lru_x, lru_x, lru_gate, *params)


def _mix_out_kernel(y_ref, bonus_ref, g_ref, lru_ref, x_ref, lw_ref, lb_ref, ones_ref, wo_ref, o_ref, *, width):
    ones_bd = ones_ref[...]
    y = y_ref[...]
    yc = y - _seg_sum(y, ones_bd) * (1.0 / HEAD_DIM)
    var = _seg_sum(yc * yc, ones_bd) * (1.0 / HEAD_DIM)
    yn = yc * lax.rsqrt(var + GN_EPS) * lw_ref[...] + lb_ref[...]
    rwkv_out = ((yn + bonus_ref[...]) * g_ref[...]).astype(BF16)
    out = x_ref[...] + jnp.dot(rwkv_out, wo_ref[0:width, :], preferred_element_type=F32)
    o_ref[...] = out + jnp.dot(lru_ref[...].astype(BF16), wo_ref[width:, :], preferred_element_type=F32)


def _mix_out(y, bonus, g, lru_out, x, lnx_w, lnx_b, ones_bd, w_out, tm):
    T, W = y.shape
    D = x.shape[1]
    row = lambda a: a.reshape(1, -1)
    full = lambda a: pl.BlockSpec(a.shape, lambda i: (0,) * a.ndim)
    params = [row(lnx_w), row(lnx_b), ones_bd, w_out]
    act = pl.BlockSpec((tm, W), lambda i: (i, 0))
    return pl.pallas_call(
        functools.partial(_mix_out_kernel, width=W),
        grid=(T // tm,),
        in_specs=[act, act, act, act, pl.BlockSpec((tm, D), lambda i: (i, 0))] + [full(a) for a in params],
        out_specs=pl.BlockSpec((tm, D), lambda i: (i, 0)),
        out_shape=jax.ShapeDtypeStruct((T, D), F32),
        compiler_params=_cparams("arbitrary"),
        name="mix_out",
    )(y, bonus, g, lru_out, x, *params)


def _matmul_res_kernel(a_ref, w_ref, x_ref, o_ref):
    o_ref[...] = x_ref[...] + jnp.dot(a_ref[...].astype(BF16), w_ref[...], preferred_element_type=F32)


def _matmul_res(a, w, x, tm):
    T, K = a.shape
    D = w.shape[1]
    return pl.pallas_call(
        _matmul_res_kernel,
        grid=(T // tm,),
        in_specs=[pl.BlockSpec((tm, K), lambda i: (i, 0)),
                  pl.BlockSpec((K, D), lambda i: (0, 0)),
                  pl.BlockSpec((tm, D), lambda i: (i, 0))],
        out_specs=pl.BlockSpec((tm, D), lambda i: (i, 0)),
        out_shape=jax.ShapeDtypeStruct((T, D), F32),
        compiler_params=_cparams("arbitrary"),
        name="matmul_res",
    )(a, w, x)


def _ffn_kernel(x_ref, g_ref, wg_ref, wu_ref, wd_ref, gf_ref, o_ref, *, th, final_norm):
    x = x_ref[...]
    h = _rms(x, g_ref[...]).astype(BF16)
    acc = x
    hidden = wg_ref.shape[1]
    for c in range(0, hidden, th):
        gt = jnp.dot(h, wg_ref[:, c:c + th], preferred_element_type=F32)
        ut = jnp.dot(h, wu_ref[:, c:c + th], preferred_element_type=F32)
        act = (gt * jax.nn.sigmoid(gt) * ut).astype(BF16)
        acc = acc + jnp.dot(act, wd_ref[c:c + th, :], preferred_element_type=F32)
    if final_norm:
        acc = _rms(acc, gf_ref[...])
    o_ref[...] = acc


def _ffn(x, g, wg, wu, wd, g_final, final_norm, tm, th):
    T, D = x.shape
    H = wg.shape[1]
    assert H % th == 0 and T % tm == 0
    resident = lambda a: pl.BlockSpec(a.shape, lambda i: (0,) * a.ndim, pipeline_mode=pl.Buffered(1))
    return pl.pallas_call(
        functools.partial(_ffn_kernel, th=th, final_norm=final_norm),
        grid=(T // tm,),
        in_specs=[pl.BlockSpec((tm, D), lambda i: (i, 0)),
                  pl.BlockSpec((1, D), lambda i: (0, 0)),
                  resident(wg), resident(wu), resident(wd),
                  pl.BlockSpec((1, D), lambda i: (0, 0))],
        out_specs=pl.BlockSpec((tm, D), lambda i: (i, 0)),
        out_shape=jax.ShapeDtypeStruct((T, D), F32),
        compiler_params=_cparams("arbitrary"),
        name="ffn",
    )(x, g.reshape(1, D), wg, wu, wd, g_final.reshape(1, D))


MASKED = -1e30
F32_MAX = 3.4028234663852886e38


def _rope(x, cos2, sin2):
    lane = lax.broadcasted_iota(I32, x.shape, 1)
    first_half = (lane % HEAD_DIM) < HEAD_DIM // 2
    partner = jnp.where(first_half, pltpu.roll(x, LANES - HEAD_DIM // 2, axis=1),
                        pltpu.roll(x, HEAD_DIM // 2, axis=1))
    return x * cos2 + partner * sin2


def _count_ge(sc_scr, thr):
    return jnp.sum((sc_scr[...] >= thr).astype(I32), axis=1, keepdims=True)


def _dsa_kernel(q_ref, iq_ref, qm_ref, kvm_ref, cosk_ref, sink_ref, cosq_ref, sinq_ref, prev_ref, o_ref,
                kTb_scr, ikT_scr, v_scr, qs_scr, iqs_scr, sc_scr, *, sk, q0, topk, heads, iheads, kt, hg):
    del prev_ref
    j = pl.program_id(1)
    QB, HD = Q_BLOCK, HEAD_DIM
    scale = HD ** -0.5

    @pl.when(j == 0)
    def _():
        kv = kvm_ref[...]
        kr_t = _rope(kv[:, 0:LANES], cosk_ref[...], sink_ref[...]).T
        kTb_scr[0:HD] = kr_t[0:HD].astype(BF16)
        b1, b2 = _split_bf16(kr_t[HD:2 * HD], 2)
        ikT_scr[0 * HD:1 * HD] = b1
        ikT_scr[1 * HD:2 * HD] = b1
        ikT_scr[2 * HD:3 * HD] = b2
        ikT_scr[3 * HD:4 * HD] = b2
        lane = lax.broadcasted_iota(I32, (sk, LANES), 1)
        v_scr[...] = jnp.where(lane < HD, kv[:, LANES:2 * LANES], jnp.where(lane == HD, 1.0, 0.0)).astype(BF16)
        eye = (lax.broadcasted_iota(I32, (QB, QB), 0) == lax.broadcasted_iota(I32, (QB, QB), 1)).astype(BF16)
        for h in range(heads):
            qs_scr[h * QB:(h + 1) * QB, HD:HD + QB] = eye

    cq, sq = cosq_ref[...], sinq_ref[...]
    q = q_ref[...]
    for c in range(heads // 2):
        ch = (_rope(q[:, c * LANES:(c + 1) * LANES], cq, sq) * scale).astype(BF16)
        qs_scr[(2 * c) * QB:(2 * c + 1) * QB, 0:HD] = ch[:, 0:HD]
        qs_scr[(2 * c + 1) * QB:(2 * c + 2) * QB, 0:HD] = ch[:, HD:2 * HD]
    iq = iq_ref[...]
    for c in range(iheads // 2):
        a1, a2 = _split_bf16(_rope(iq[:, c * LANES:(c + 1) * LANES], cq, sq) * scale, 2)
        for hh in range(2):
            rows = slice((2 * c + hh) * QB, (2 * c + hh + 1) * QB)
            a1h, a2h = a1[:, hh * HD:(hh + 1) * HD], a2[:, hh * HD:(hh + 1) * HD]
            iqs_scr[rows, 0 * HD:1 * HD] = a1h
            iqs_scr[rows, 1 * HD:2 * HD] = a2h
            iqs_scr[rows, 2 * HD:3 * HD] = a1h
            iqs_scr[rows, 3 * HD:4 * HD] = a2h

    iw = qm_ref[:, LANES + HD:LANES + HD + iheads] * (iheads ** -0.5)
    q_chunk = (q0 + j * QB + lax.broadcasted_iota(I32, (QB, kt), 0)) // CHUNK
    for t in range(sk // kt):
        logit = jnp.dot(iqs_scr[...], ikT_scr[:, t * kt:(t + 1) * kt], preferred_element_type=F32)
        score = None
        for h in range(iheads):
            term = iw[:, h:h + 1] * jnp.maximum(logit[h * QB:(h + 1) * QB], 0.0)
            score = term if score is None else score + term
        k_chunk = (t * kt + lax.broadcasted_iota(I32, (QB, kt), 1)) // CHUNK
        sc_scr[:, t * kt:(t + 1) * kt] = jnp.where(k_chunk <= q_chunk, score, -jnp.inf)

    s = sc_scr[...]
    n_adm = jnp.minimum((q_chunk[:, 0:1] + 1) * CHUNK, sk)
    take_all = n_adm <= topk
    rmax = jnp.max(s, axis=1, keepdims=True)
    rmin = jnp.min(jnp.where(s == -jnp.inf, jnp.inf, s), axis=1, keepdims=True)
    c_max = _count_ge(sc_scr, rmax)
    top_tied = c_max >= topk
    lo0 = jnp.where(top_tied, rmax, rmin)
    cnt0 = jnp.where(top_tied, c_max, n_adm)
    done0 = (take_all | top_tied | (cnt0 == topk)).astype(I32)

    def bisect(state):
        lo, hi, cnt, done = state
        mid = 0.5 * lo + 0.5 * hi
        inside = (mid > lo) & (mid < hi)
        c = _count_ge(sc_scr, mid)
        ge = c >= topk
        move = inside & (done == 0)
        lo = jnp.where(move & ge, mid, lo)
        cnt = jnp.where(move & ge, c, cnt)
        hi = jnp.where(move & jnp.logical_not(ge), mid, hi)
        done = jnp.where(inside & (cnt != topk), done, 1)
        return lo, hi, cnt, done

    def two_steps(state):
        return bisect(bisect(state))

    def pending(state):
        return jnp.min(state[3]) == 0

    lo, _, cnt, _ = lax.while_loop(pending, two_steps, (lo0, rmax, cnt0, done0))
    thr = jnp.where(take_all, -F32_MAX, lo)
    tied = jnp.max((jnp.logical_not(take_all) & (cnt > topk)).astype(I32)) > 0

    @pl.when(jnp.logical_not(tied))
    def _():
        kTb_scr[HD:HD + QB, :] = jnp.where(sc_scr[...] >= thr, 0.0, MASKED).astype(BF16)

    @pl.when(tied)
    def _():
        need = (topk - jnp.sum((sc_scr[...] > thr).astype(I32), axis=1, keepdims=True)).astype(F32)
        tri = (lax.broadcasted_iota(I32, (kt, kt), 0) <= lax.broadcasted_iota(I32, (kt, kt), 1)).astype(BF16)
        before = jnp.zeros((QB, 1), F32)
        for t in range(sk // kt):
            st = sc_scr[:, t * kt:(t + 1) * kt]
            eq = st == thr
            rank = before + jnp.dot(eq.astype(BF16), tri, preferred_element_type=F32)
            sel = (st > thr) | (eq & (rank <= need))
            kTb_scr[HD:HD + QB, t * kt:(t + 1) * kt] = jnp.where(sel, 0.0, MASKED).astype(BF16)
            before = rank[:, kt - 1:kt]

    R = hg * QB
    logits = lambda g: jnp.dot(qs_scr[g * R:(g + 1) * R, :], kTb_scr[...], preferred_element_type=F32)
    lg = logits(0)
    for g in range(heads // hg):
        lg_next = logits(g + 1) if g + 1 < heads // hg else None
        p = jnp.exp(lg - jnp.max(lg, axis=-1, keepdims=True)).astype(BF16)
        o = jnp.dot(p, v_scr[...], preferred_element_type=F32)
        o = o[:, 0:HD] / o[:, HD:HD + 1]
        for hh in range(hg):
            h = g * hg + hh
            o_ref[:, h * HD:(h + 1) * HD] = o[hh * QB:(hh + 1) * QB]
        lg = lg_next


def _dsa_group(q, iq, kvm, cos2, sin2, prev_out, batch, seq, g, gq, topk, heads, iheads):
    QB, HD = Q_BLOCK, HEAD_DIM
    nq = seq // QB
    sk = (g + 1) * gq * QB
    kt = min(sk, 512)
    hg = 2
    misc = kvm.shape[1]
    kvm3 = kvm.reshape(batch, seq, misc)
    qrow = lambda b, j: (b * nq + g * gq + j, 0)
    kernel = functools.partial(_dsa_kernel, sk=sk, q0=g * gq * QB, topk=topk, heads=heads, iheads=iheads,
                               kt=kt, hg=hg)
    args = [q, iq, kvm, kvm3, cos2, sin2, cos2, sin2, prev_out]
    return pl.pallas_call(
        kernel,
        grid=(batch, gq),
        in_specs=[pl.BlockSpec((QB, heads * HD), qrow),
                  pl.BlockSpec((QB, iheads * HD), qrow),
                  pl.BlockSpec((QB, misc), qrow),
                  pl.BlockSpec((None, sk, misc), lambda b, j: (b, 0, 0)),
                  pl.BlockSpec((sk, LANES), lambda b, j: (0, 0)),
                  pl.BlockSpec((sk, LANES), lambda b, j: (0, 0)),
                  pl.BlockSpec((QB, LANES), lambda b, j: (g * gq + j, 0)),
                  pl.BlockSpec((QB, LANES), lambda b, j: (g * gq + j, 0)),
                  pl.BlockSpec(memory_space=pl.ANY)],
        out_specs=pl.BlockSpec((QB, heads * HD), qrow),
        out_shape=jax.ShapeDtypeStruct((batch * seq, heads * HD), F32),
        input_output_aliases={len(args) - 1: 0},
        scratch_shapes=[pltpu.VMEM((HD + QB, sk), BF16),
                        pltpu.VMEM((4 * HD, sk), BF16),
                        pltpu.VMEM((sk, LANES), BF16),
                        pltpu.VMEM((heads * QB, HD + QB), BF16),
                        pltpu.VMEM((iheads * QB, 4 * HD), BF16),
                        pltpu.VMEM((QB, sk), F32)],
        compiler_params=_cparams("arbitrary", "arbitrary"),
        name=f"dsa_g{g}",
    )(*args)


def _rope_tables(seq):
    half = HEAD_DIM // 2
    inv_freq = ROPE_THETA ** (-jnp.arange(0, HEAD_DIM, 2, dtype=F32) / HEAD_DIM)
    ang = jnp.arange(seq, dtype=F32)[:, None] * inv_freq[None, :]
    cos, sin = jnp.cos(ang), jnp.sin(ang)
    cos2 = jnp.tile(cos, (1, LANES // half))
    sin2 = jnp.tile(jnp.concatenate([-sin, sin], axis=1), (1, LANES // HEAD_DIM))
    return cos2, sin2


def _block_diag(w):
    h, n, _ = w.shape
    eye = jnp.eye(h, dtype=w.dtype)
    return (eye[:, None, :, None] * w[:, :, None, :]).reshape(h * n, h * n)


def kernel(x, norm_mix, w_in0, mu_shift, w_decay0, w_decay_up, a_bias, w_a_up, w_g_up, k_k, k_a, r_k, lnx_w, lnx_b, conv_w, conv_b, w_rgate, b_rgate, w_igate, b_igate, lru_lambda, w_out0, w_in1, w_out1, norm_ffn, ffn_gate, ffn_up, ffn_down, norm_final):
    B, S, D = x.shape
    T = B * S
    HD = HEAD_DIM
    W = w_decay0.shape[-1]
    H = W // HD
    dlora, alora = w_decay_up.shape[1], w_a_up.shape[1]
    assert dlora + alora == LANES and w_g_up.shape[1] == LANES
    x2 = x.reshape(T, D)
    tm = min(512, S)

    rwkv_cols = 3 * W + 2 * LANES
    rwkv_p, lru_x, lru_gate = _norm_matmul(x2, norm_mix[0], w_in0[0].astype(BF16), (rwkv_cols, W, W), tm)
    ones_bd = _block_diag(jnp.ones((H, HD, HD), BF16))
    wdu_pad = jnp.concatenate([w_decay_up[0], jnp.zeros((alora, W), F32)], axis=0).astype(BF16)
    wau_pad = jnp.concatenate([jnp.zeros((dlora, W), F32), w_a_up[0]], axis=0).astype(BF16)
    r, w, k, v, nkk, kka, bonus, g = _rwkv_prep(
        rwkv_p, mu_shift[0], w_decay0[0], wdu_pad, a_bias[0], wau_pad, w_g_up[0].astype(BF16), k_k[0], k_a[0],
        r_k[0].reshape(-1), ones_bd, S, min(256, S))
    to_scan = lambda t: t.reshape(B, S, H, HD).transpose(1, 3, 0, 2).reshape(S, HD, B * H)
    y = _rwkv_scan(*(to_scan(t) for t in (r, w, k, v, nkk, kka)), steps=16)
    y = y.reshape(S, HD, B, H).transpose(2, 0, 3, 1).reshape(T, W)
    lru_out = _lru(lru_x, lru_gate, conv_w[0], conv_b[0], _block_diag(w_rgate[0]).astype(BF16),
                   b_rgate[0].reshape(-1), _block_diag(w_igate[0]).astype(BF16), b_igate[0].reshape(-1),
                   lru_lambda[0], B, S, min(256, S))
    x2 = _mix_out(y, bonus, g, lru_out, x2, lnx_w[0], lnx_b[0], ones_bd, w_out0[0].astype(BF16), tm)
    x2 = _ffn(x2, norm_ffn[0], ffn_gate[0].astype(BF16), ffn_up[0].astype(BF16), ffn_down[0].astype(BF16),
              norm_final, False, tm, 256)

    heads = w_out1.shape[1] // HD
    iheads = (w_in1.shape[2] - heads * HD - 3 * HD) // (HD + 1)
    c0 = heads * HD
    wq, wk, wv = w_in1[0][:, :c0], w_in1[0][:, c0:c0 + HD], w_in1[0][:, c0 + HD:c0 + 2 * HD]
    wiq = w_in1[0][:, c0 + 2 * HD:c0 + 2 * HD + iheads * HD]
    wik = w_in1[0][:, c0 + 2 * HD + iheads * HD:c0 + 3 * HD + iheads * HD]
    wiw = w_in1[0][:, c0 + 3 * HD + iheads * HD:]
    misc = 2 * LANES
    pad = jnp.zeros((D, misc - 3 * HD - iheads), F32)
    w1 = jnp.concatenate([wq, wiq, wk, wik, wv, wiw, pad], axis=1).astype(BF16)
    q, iq, kvm = _norm_matmul(x2, norm_mix[1], w1, (heads * HD, iheads * HD, misc), tm)
    cos2, sin2 = _rope_tables(S)
    topk = min(TOPK_MAX, S // 4)
    nq = S // Q_BLOCK
    gq = 4 if nq % 4 == 0 and nq > 4 else 1
    attn = jnp.zeros((T, c0), F32)
    for gi in range(nq // gq):
        attn = _dsa_group(q, iq, kvm, cos2, sin2, attn, B, S, gi, gq, topk, heads, iheads)
    x2 = _matmul_res(attn, w_out1[0].astype(BF16), x2, tm)
    x2 = _ffn(x2, norm_ffn[1], ffn_gate[1].astype(BF16), ffn_up[1].astype(BF16), ffn_down[1].astype(BF16),
              norm_final, True, tm, 256)
    return x2.reshape(B, S, D)
```

```python
import functools

import jax
import jax.numpy as jnp
from jax import lax
from jax.experimental import pallas as pl
from jax.experimental.pallas import tpu as pltpu

F32 = jnp.float32
BF16 = jnp.bfloat16
I32 = jnp.int32

NORM_EPS = 1e-6
GN_EPS = 64e-5
LRU_C = 8.0
ROPE_THETA = 10000.0
HEAD_DIM = 64
CHUNK = 64
Q_BLOCK = 128
TOPK_MAX = 256
CONV_WIDTH = 4
LANES = 128
SUBLANES = 8
VMEM_LIMIT_BYTES = 56 * 1024 * 1024
INT_MIN = -(2 ** 31)


def _cparams(*semantics):
    return pltpu.CompilerParams(dimension_semantics=semantics, vmem_limit_bytes=VMEM_LIMIT_BYTES)


def _rms(x, g):
    return x * lax.rsqrt(jnp.mean(x * x, axis=-1, keepdims=True) + NORM_EPS) * g


def _split_bf16(x, n):
    parts = []
    for _ in range(n):
        p = x.astype(BF16)
        parts.append(p)
        x = x - p.astype(F32)
    return parts


def _seg_sum(x, ones_bd):
    out = None
    for p in _split_bf16(x, 3):
        d = jnp.dot(p, ones_bd, preferred_element_type=F32)
        out = d if out is None else out + d
    return out


def _softplus(x):
    return jnp.maximum(x, 0.0) + jnp.log(1.0 + jnp.exp(-jnp.abs(x)))


def _gelu_tanh(x):
    cdf = 0.5 * (1.0 + jnp.tanh(0.7978845608028654 * (x + 0.044715 * (x * x * x))))
    return x * cdf


def _norm_matmul_kernel(x_ref, g_ref, w_ref, *o_refs, splits, n_chunk):
    h = _rms(x_ref[...], g_ref[...]).astype(BF16)
    off = 0
    for o_ref, n in zip(o_refs, splits):
        for c in range(0, n, n_chunk):
            cw = min(n_chunk, n - c)
            o_ref[:, c:c + cw] = jnp.dot(h, w_ref[:, off + c:off + c + cw], preferred_element_type=F32)
        off += n


def _norm_matmul(x, g, w, splits, tm):
    T, D = x.shape
    N = w.shape[1]
    assert sum(splits) == N and T % tm == 0
    return pl.pallas_call(
        functools.partial(_norm_matmul_kernel, splits=splits, n_chunk=512),
        grid=(T // tm,),
        in_specs=[pl.BlockSpec((tm, D), lambda i: (i, 0)),
                  pl.BlockSpec((1, D), lambda i: (0, 0)),
                  pl.BlockSpec((D, N), lambda i: (0, 0))],
        out_specs=[pl.BlockSpec((tm, n), lambda i: (i, 0)) for n in splits],
        out_shape=[jax.ShapeDtypeStruct((T, n), F32) for n in splits],
        compiler_params=_cparams("arbitrary"),
        name="norm_matmul",
    )(x, g.reshape(1, D), w)


def _rwkv_prep_kernel(p_ref, halo_ref, mu_ref, dec0_ref, wdu_ref, ab_ref, wau_ref, wgu_ref, kk_ref, ka_ref,
                      rk_ref, ones_ref, r_o, w_o, k_o, v_o, nkk_o, kka_o, bonus_o, g_o, *, tm, seq, width):
    i = pl.program_id(0)
    p = p_ref[...]
    at_seq_start = (i * tm) % seq == 0
    prev_row = jnp.where(at_seq_start, 0.0, halo_ref[SUBLANES - 1:SUBLANES, :])
    row = lax.broadcasted_iota(I32, p.shape, 0)
    prev = jnp.where(row == 0, prev_row, pltpu.roll(p, 1, axis=0))
    ps = p + (prev - p) * mu_ref[...]
    W = width
    r, k, v = ps[:, 0:W], ps[:, W:2 * W], ps[:, 2 * W:3 * W]
    lora = ps[:, 3 * W:3 * W + LANES]
    gd = ps[:, 3 * W + LANES:3 * W + 2 * LANES]
    ones_bd = ones_ref[...]

    z = dec0_ref[...] + jnp.dot(jnp.tanh(lora).astype(BF16), wdu_ref[...], preferred_element_type=F32)
    w_log = -_softplus(-z) - 0.5
    w_o[...] = jnp.exp(-jnp.exp(w_log))
    a = jax.nn.sigmoid(ab_ref[...] + jnp.dot(lora.astype(BF16), wau_ref[...], preferred_element_type=F32))
    g_o[...] = jnp.dot(jax.nn.sigmoid(gd).astype(BF16), wgu_ref[...], preferred_element_type=F32)
    kk = k * kk_ref[...]
    kk = kk * lax.rsqrt(jnp.maximum(_seg_sum(kk * kk, ones_bd), 1e-24))
    k_mod = k * (1.0 + (a - 1.0) * ka_ref[...])
    r_o[...] = r
    k_o[...] = k_mod
    v_o[...] = v
    nkk_o[...] = -kk
    kka_o[...] = kk * a
    bonus_o[...] = _seg_sum(r * k_mod * rk_ref[...], ones_bd) * v


def _rwkv_prep(rwkv_p, mu, dec0, wdu_pad, a_bias, wau_pad, wgu, k_k, k_a, r_k, ones_bd, seq, tm):
    T, P = rwkv_p.shape
    W = dec0.shape[-1]
    assert T % tm == 0 and seq % tm == 0 and tm % SUBLANES == 0
    row = lambda a: a.reshape(1, -1)
    full = lambda a: pl.BlockSpec(a.shape, lambda i: (0,) * a.ndim)
    params = [row(mu), row(dec0), wdu_pad, row(a_bias), wau_pad, wgu, row(k_k), row(k_a), row(r_k), ones_bd]
    return pl.pallas_call(
        functools.partial(_rwkv_prep_kernel, tm=tm, seq=seq, width=W),
        grid=(T // tm,),
        in_specs=[pl.BlockSpec((tm, P), lambda i: (i, 0)),
                  pl.BlockSpec((SUBLANES, P), lambda i: (jnp.maximum(i * (tm // SUBLANES) - 1, 0), 0))]
                 + [full(a) for a in params],
        out_specs=[pl.BlockSpec((tm, W), lambda i: (i, 0))] * 8,
        out_shape=[jax.ShapeDtypeStruct((T, W), F32)] * 8,
        compiler_params=_cparams("arbitrary"),
        name="rwkv_prep",
    )(rwkv_p, rwkv_p, *params)


def _rwkv_scan_kernel(r_ref, w_ref, k_ref, v_ref, nkk_ref, kka_ref, y_ref, s_ref, *, steps, n):
    @pl.when(pl.program_id(0) == 0)
    def _():
        s_ref[...] = jnp.zeros_like(s_ref)

    def step(t, carry):
        v_t = v_ref[t]
        sa = [None, None]
        for kidx in range(n):
            term = s_ref[kidx] * nkk_ref[t, kidx:kidx + 1, :]
            sa[kidx % 2] = term if sa[kidx % 2] is None else sa[kidx % 2] + term
        sa = sa[0] + sa[1]
        y = [None, None]
        for kidx in range(n):
            s_new = (s_ref[kidx] * w_ref[t, kidx:kidx + 1, :] + sa * kka_ref[t, kidx:kidx + 1, :]
                     + v_t * k_ref[t, kidx:kidx + 1, :])
            s_ref[kidx] = s_new
            term = s_new * r_ref[t, kidx:kidx + 1, :]
            y[kidx % 2] = term if y[kidx % 2] is None else y[kidx % 2] + term
        y_ref[t] = y[0] + y[1]
        return carry

    lax.fori_loop(0, steps, step, 0)


def _rwkv_scan(r, w, k, v, nkk, kka, steps):
    S, N, L = r.shape
    assert S % steps == 0
    spec = pl.BlockSpec((steps, N, L), lambda i: (i, 0, 0))
    return pl.pallas_call(
        functools.partial(_rwkv_scan_kernel, steps=steps, n=N),
        grid=(S // steps,),
        in_specs=[spec] * 6,
        out_specs=spec,
        out_shape=jax.ShapeDtypeStruct((S, N, L), F32),
        scratch_shapes=[pltpu.VMEM((N, N, L), F32)],
        compiler_params=_cparams("arbitrary"),
        name="rwkv_scan",
    )(r, w, k, v, nkk, kka)


def _lru_kernel(x_ref, halo_ref, gate_ref, cw_ref, cb_ref, wr_ref, br_ref, wi_ref, bi_ref, lam_ref, o_ref,
                h_scr, *, ts):
    j = pl.program_id(1)

    @pl.when(j == 0)
    def _():
        h_scr[...] = jnp.zeros_like(h_scr)

    x = x_ref[...]
    halo = jnp.where(j == 0, 0.0, halo_ref[...])
    row8 = lax.broadcasted_iota(I32, halo.shape, 0)
    xc = cb_ref[...] + x * cw_ref[CONV_WIDTH - 1:CONV_WIDTH, :]
    for d in range(1, CONV_WIDTH):
        rolled = pltpu.roll(x, d, axis=0)
        head = jnp.where(row8 < d, pltpu.roll(halo, d, axis=0), rolled[0:SUBLANES])
        shifted = jnp.concatenate([head, rolled[SUBLANES:]], axis=0)
        xc = xc + shifted * cw_ref[CONV_WIDTH - 1 - d:CONV_WIDTH - d, :]

    xcb = xc.astype(BF16)
    rg = jax.nn.sigmoid(jnp.dot(xcb, wr_ref[...], preferred_element_type=F32) + br_ref[...])
    ig = jax.nn.sigmoid(jnp.dot(xcb, wi_ref[...], preferred_element_type=F32) + bi_ref[...])
    log_a = -LRU_C * rg * _softplus(-lam_ref[...])
    a_cum = jnp.exp(log_a)
    th = jnp.tanh(log_a)
    x_cum = xc * ig * jnp.sqrt(-2.0 * th / (1.0 - th))

    row = lax.broadcasted_iota(I32, x.shape, 0)
    d = 1
    while d < ts:
        keep = row >= d
        a_prev = jnp.where(keep, pltpu.roll(a_cum, d, axis=0), 1.0)
        x_prev = jnp.where(keep, pltpu.roll(x_cum, d, axis=0), 0.0)
        x_cum = a_cum * x_prev + x_cum
        a_cum = a_cum * a_prev
        d *= 2
    h = x_cum + a_cum * h_scr[...]
    h_scr[...] = h[ts - 1:ts, :]
    o_ref[...] = h * _gelu_tanh(gate_ref[...])


def _lru(lru_x, lru_gate, conv_w, conv_b, wr_bd, b_r, wi_bd, b_i, lam, batch, seq, ts):
    T, W = lru_x.shape
    assert seq % ts == 0 and ts % SUBLANES == 0
    nt = seq // ts
    row = lambda a: a.reshape(1, -1)
    full = lambda a: pl.BlockSpec(a.shape, lambda b, j: (0,) * a.ndim)
    params = [conv_w, row(conv_b), wr_bd, row(b_r), wi_bd, row(b_i), row(lam)]
    return pl.pallas_call(
        functools.partial(_lru_kernel, ts=ts),
        grid=(batch, nt),
        in_specs=[pl.BlockSpec((ts, W), lambda b, j: (b * nt + j, 0)),
                  pl.BlockSpec((SUBLANES, W),
                               lambda b, j: (jnp.maximum((b * nt + j) * (ts // SUBLANES) - 1, 0), 0)),
                  pl.BlockSpec((ts, W), lambda b, j: (b * nt + j, 0))]
                 + [full(a) for a in params],
        out_specs=pl.BlockSpec((ts, W), lambda b, j: (b * nt + j, 0)),
        out_shape=jax.ShapeDtypeStruct((T, W), F32),
        scratch_shapes=[pltpu.VMEM((1, W), F32)],
        compiler_params=_cparams("arbitrary", "arbitrary"),
        name="rg_lru",
    )(lru_x, lru_x, lru_gate, *params)


def _mix_out_kernel(y_ref, bonus_ref, g_ref, lru_ref, x_ref, lw_ref, lb_ref, ones_ref, wo_ref, o_ref, *, width):
    ones_bd = ones_ref[...]
    y = y_ref[...]
    yc = y - _seg_sum(y, ones_bd) * (1.0 / HEAD_DIM)
    var = _seg_sum(yc * yc, ones_bd) * (1.0 / HEAD_DIM)
    yn = yc * lax.rsqrt(var + GN_EPS) * lw_ref[...] + lb_ref[...]
    rwkv_out = ((yn + bonus_ref[...]) * g_ref[...]).astype(BF16)
    out = x_ref[...] + jnp.dot(rwkv_out, wo_ref[0:width, :], preferred_element_type=F32)
    o_ref[...] = out + jnp.dot(lru_ref[...].astype(BF16), wo_ref[width:, :], preferred_element_type=F32)


def _mix_out(y, bonus, g, lru_out, x, lnx_w, lnx_b, ones_bd, w_out, tm):
    T, W = y.shape
    D = x.shape[1]
    row = lambda a: a.reshape(1, -1)
    full = lambda a: pl.BlockSpec(a.shape, lambda i: (0,) * a.ndim)
    params = [row(lnx_w), row(lnx_b), ones_bd, w_out]
    act = pl.BlockSpec((tm, W), lambda i: (i, 0))
    return pl.pallas_call(
        functools.partial(_mix_out_kernel, width=W),
        grid=(T // tm,),
        in_specs=[act, act, act, act, pl.BlockSpec((tm, D), lambda i: (i, 0))] + [full(a) for a in params],
        out_specs=pl.BlockSpec((tm, D), lambda i: (i, 0)),
        out_shape=jax.ShapeDtypeStruct((T, D), F32),
        compiler_params=_cparams("arbitrary"),
        name="mix_out",
    )(y, bonus, g, lru_out, x, *params)


def _matmul_res_kernel(a_ref, w_ref, x_ref, o_ref):
    o_ref[...] = x_ref[...] + jnp.dot(a_ref[...].astype(BF16), w_ref[...], preferred_element_type=F32)


def _matmul_res(a, w, x, tm):
    T, K = a.shape
    D = w.shape[1]
    return pl.pallas_call(
        _matmul_res_kernel,
        grid=(T // tm,),
        in_specs=[pl.BlockSpec((tm, K), lambda i: (i, 0)),
                  pl.BlockSpec((K, D), lambda i: (0, 0)),
                  pl.BlockSpec((tm, D), lambda i: (i, 0))],
        out_specs=pl.BlockSpec((tm, D), lambda i: (i, 0)),
        out_shape=jax.ShapeDtypeStruct((T, D), F32),
        compiler_params=_cparams("arbitrary"),
        name="matmul_res",
    )(a, w, x)


def _ffn_kernel(x_ref, g_ref, wg_ref, wu_ref, wd_ref, gf_ref, o_ref, *, th, final_norm):
    x = x_ref[...]
    h = _rms(x, g_ref[...]).astype(BF16)
    acc = x
    hidden = wg_ref.shape[1]
    for c in range(0, hidden, th):
        gt = jnp.dot(h, wg_ref[:, c:c + th], preferred_element_type=F32)
        ut = jnp.dot(h, wu_ref[:, c:c + th], preferred_element_type=F32)
        act = (gt * jax.nn.sigmoid(gt) * ut).astype(BF16)
        acc = acc + jnp.dot(act, wd_ref[c:c + th, :], preferred_element_type=F32)
    if final_norm:
        acc = _rms(acc, gf_ref[...])
    o_ref[...] = acc


def _ffn(x, g, wg, wu, wd, g_final, final_norm, tm, th):
    T, D = x.shape
    H = wg.shape[1]
    assert H % th == 0 and T % tm == 0
    resident = lambda a: pl.BlockSpec(a.shape, lambda i: (0,) * a.ndim, pipeline_mode=pl.Buffered(1))
    return pl.pallas_call(
        functools.partial(_ffn_kernel, th=th, final_norm=final_norm),
        grid=(T // tm,),
        in_specs=[pl.BlockSpec((tm, D), lambda i: (i, 0)),
                  pl.BlockSpec((1, D), lambda i: (0, 0)),
                  resident(wg), resident(wu), resident(wd),
                  pl.BlockSpec((1, D), lambda i: (0, 0))],
        out_specs=pl.BlockSpec((tm, D), lambda i: (i, 0)),
        out_shape=jax.ShapeDtypeStruct((T, D), F32),
        compiler_params=_cparams("arbitrary"),
        name="ffn",
    )(x, g.reshape(1, D), wg, wu, wd, g_final.reshape(1, D))


MASKED = -1e30
F32_MAX = 3.4028234663852886e38


def _rope(x, cos2, sin2):
    lane = lax.broadcasted_iota(I32, x.shape, 1)
    first_half = (lane % HEAD_DIM) < HEAD_DIM // 2
    partner = jnp.where(first_half, pltpu.roll(x, LANES - HEAD_DIM // 2, axis=1),
                        pltpu.roll(x, HEAD_DIM // 2, axis=1))
    return x * cos2 + partner * sin2


def _count_ge(sc_scr, thr):
    return jnp.sum((sc_scr[...] >= thr).astype(I32), axis=1, keepdims=True)


def _dsa_kernel(q_ref, iq_ref, qm_ref, kvm_ref, cosk_ref, sink_ref, cosq_ref, sinq_ref, prev_ref, o_ref,
                kTb_scr, ikT_scr, v_scr, qs_scr, iqs_scr, sc_scr, *, sk, q0, topk, heads, iheads, kt, hg):
    del prev_ref
    j = pl.program_id(1)
    QB, HD = Q_BLOCK, HEAD_DIM
    scale = HD ** -0.5

    @pl.when(j == 0)
    def _():
        kv = kvm_ref[...]
        kr_t = _rope(kv[:, 0:LANES], cosk_ref[...], sink_ref[...]).T
        kTb_scr[0:HD] = kr_t[0:HD].astype(BF16)
        b1, b2 = _split_bf16(kr_t[HD:2 * HD], 2)
        ikT_scr[0 * HD:1 * HD] = b1
        ikT_scr[1 * HD:2 * HD] = b1
        ikT_scr[2 * HD:3 * HD] = b2
        ikT_scr[3 * HD:4 * HD] = b2
        lane = lax.broadcasted_iota(I32, (sk, LANES), 1)
        v_scr[...] = jnp.where(lane < HD, kv[:, LANES:2 * LANES], jnp.where(lane == HD, 1.0, 0.0)).astype(BF16)
        eye = (lax.broadcasted_iota(I32, (QB, QB), 0) == lax.broadcasted_iota(I32, (QB, QB), 1)).astype(BF16)
        for h in range(heads):
            qs_scr[h * QB:(h + 1) * QB, HD:HD + QB] = eye

    cq, sq = cosq_ref[...], sinq_ref[...]
    q = q_ref[...]
    for c in range(heads // 2):
        ch = (_rope(q[:, c * LANES:(c + 1) * LANES], cq, sq) * scale).astype(BF16)
        qs_scr[(2 * c) * QB:(2 * c + 1) * QB, 0:HD] = ch[:, 0:HD]
        qs_scr[(2 * c + 1) * QB:(2 * c + 2) * QB, 0:HD] = ch[:, HD:2 * HD]
    iq = iq_ref[...]
    for c in range(iheads // 2):
        a1, a2 = _split_bf16(_rope(iq[:, c * LANES:(c + 1) * LANES], cq, sq) * scale, 2)
        for hh in range(2):
            rows = slice((2 * c + hh) * QB, (2 * c + hh + 1) * QB)
            a1h, a2h = a1[:, hh * HD:(hh + 1) * HD], a2[:, hh * HD:(hh + 1) * HD]
            iqs_scr[rows, 0 * HD:1 * HD] = a1h
            iqs_scr[rows, 1 * HD:2 * HD] = a2h
            iqs_scr[rows, 2 * HD:3 * HD] = a1h
            iqs_scr[rows, 3 * HD:4 * HD] = a2h

    iw = qm_ref[:, LANES + HD:LANES + HD + iheads] * (iheads ** -0.5)
    q_chunk = (q0 + j * QB + lax.broadcasted_iota(I32, (QB, kt), 0)) // CHUNK
    for t in range(sk // kt):
        logit = jnp.dot(iqs_scr[...], ikT_scr[:, t * kt:(t + 1) * kt], preferred_element_type=F32)
        score = None
        for h in range(iheads):
            term = iw[:, h:h + 1] * jnp.maximum(logit[h * QB:(h + 1) * QB], 0.0)
            score = term if score is None else score + term
        k_chunk = (t * kt + lax.broadcasted_iota(I32, (QB, kt), 1)) // CHUNK
        sc_scr[:, t * kt:(t + 1) * kt] = jnp.where(k_chunk <= q_chunk, score, -jnp.inf)

    s = sc_scr[...]
    n_adm = jnp.minimum((q_chunk[:, 0:1] + 1) * CHUNK, sk)
    take_all = n_adm <= topk
    rmax = jnp.max(s, axis=1, keepdims=True)
    rmin = jnp.min(jnp.where(s == -jnp.inf, jnp.inf, s), axis=1, keepdims=True)
    c_max = _count_ge(sc_scr, rmax)
    top_tied = c_max >= topk
    c_pos = jnp.sum((s > 0.0).astype(I32), axis=1, keepdims=True)
    c_nonneg = _count_ge(sc_scr, 0.0)
    zero_is_kth = (c_pos < topk) & (c_nonneg >= topk)
    from_zero = zero_is_kth | (c_pos >= topk)
    lo0 = jnp.where(top_tied, rmax, jnp.where(from_zero, 0.0, rmin))
    cnt0 = jnp.where(top_tied, c_max, jnp.where(from_zero, c_nonneg, n_adm))
    hi0 = jnp.where(top_tied | (c_pos >= topk), rmax, 0.0)
    done0 = (take_all | top_tied | zero_is_kth | (cnt0 == topk)).astype(I32)

    def bisect(state):
        lo, hi, cnt, done = state
        mid = 0.5 * lo + 0.5 * hi
        inside = (mid > lo) & (mid < hi)
        c = _count_ge(sc_scr, mid)
        ge = c >= topk
        move = inside & (done == 0)
        lo = jnp.where(move & ge, mid, lo)
        cnt = jnp.where(move & ge, c, cnt)
        hi = jnp.where(move & jnp.logical_not(ge), mid, hi)
        done = jnp.where(inside & (cnt != topk), done, 1)
        return lo, hi, cnt, done

    def two_steps(state):
        return bisect(bisect(state))

    def pending(state):
        return jnp.min(state[3]) == 0

    lo, _, cnt, _ = lax.while_loop(pending, two_steps, (lo0, hi0, cnt0, done0))
    thr = jnp.where(take_all, -F32_MAX, lo)
    tied = jnp.max((jnp.logical_not(take_all) & (cnt > topk)).astype(I32)) > 0

    @pl.when(jnp.logical_not(tied))
    def _():
        kTb_scr[HD:HD + QB, :] = jnp.where(sc_scr[...] >= thr, 0.0, MASKED).astype(BF16)

    @pl.when(tied)
    def _():
        need = (topk - jnp.sum((sc_scr[...] > thr).astype(I32), axis=1, keepdims=True)).astype(F32)
        tri = (lax.broadcasted_iota(I32, (kt, kt), 0) <= lax.broadcasted_iota(I32, (kt, kt), 1)).astype(BF16)
        before = jnp.zeros((QB, 1), F32)
        for t in range(sk // kt):
            st = sc_scr[:, t * kt:(t + 1) * kt]
            eq = st == thr
            rank = before + jnp.dot(eq.astype(BF16), tri, preferred_element_type=F32)
            sel = (st > thr) | (eq & (rank <= need))
            kTb_scr[HD:HD + QB, t * kt:(t + 1) * kt] = jnp.where(sel, 0.0, MASKED).astype(BF16)
            before = rank[:, kt - 1:kt]

    R = hg * QB
    logits = lambda g: jnp.dot(qs_scr[g * R:(g + 1) * R, :], kTb_scr[...], preferred_element_type=F32)
    lg = logits(0)
    for g in range(heads // hg):
        lg_next = logits(g + 1) if g + 1 < heads // hg else None
        p = jnp.exp(lg - jnp.max(lg, axis=-1, keepdims=True)).astype(BF16)
        o = jnp.dot(p, v_scr[...], preferred_element_type=F32)
        o = o[:, 0:HD] / o[:, HD:HD + 1]
        for hh in range(hg):
            h = g * hg + hh
            o_ref[:, h * HD:(h + 1) * HD] = o[hh * QB:(hh + 1) * QB]
        lg = lg_next


def _dsa_group(q, iq, kvm, cos2, sin2, prev_out, batch, seq, g, gq, topk, heads, iheads):
    QB, HD = Q_BLOCK, HEAD_DIM
    nq = seq // QB
    sk = (g + 1) * gq * QB
    kt = min(sk, 512)
    hg = 2
    misc = kvm.shape[1]
    kvm3 = kvm.reshape(batch, seq, misc)
    qrow = lambda b, j: (b * nq + g * gq + j, 0)
    kernel = functools.partial(_dsa_kernel, sk=sk, q0=g * gq * QB, topk=topk, heads=heads, iheads=iheads,
                               kt=kt, hg=hg)
    args = [q, iq, kvm, kvm3, cos2, sin2, cos2, sin2, prev_out]
    return pl.pallas_call(
        kernel,
        grid=(batch, gq),
        in_specs=[pl.BlockSpec((QB, heads * HD), qrow),
                  pl.BlockSpec((QB, iheads * HD), qrow),
                  pl.BlockSpec((QB, misc), qrow),
                  pl.BlockSpec((None, sk, misc), lambda b, j: (b, 0, 0)),
                  pl.BlockSpec((sk, LANES), lambda b, j: (0, 0)),
                  pl.BlockSpec((sk, LANES), lambda b, j: (0, 0)),
                  pl.BlockSpec((QB, LANES), lambda b, j: (g * gq + j, 0)),
                  pl.BlockSpec((QB, LANES), lambda b, j: (g * gq + j, 0)),
                  pl.BlockSpec(memory_space=pl.ANY)],
        out_specs=pl.BlockSpec((QB, heads * HD), qrow),
        out_shape=jax.ShapeDtypeStruct((batch * seq, heads * HD), F32),
        input_output_aliases={len(args) - 1: 0},
        scratch_shapes=[pltpu.VMEM((HD + QB, sk), BF16),
                        pltpu.VMEM((4 * HD, sk), BF16),
                        pltpu.VMEM((sk, LANES), BF16),
                        pltpu.VMEM((heads * QB, HD + QB), BF16),
                        pltpu.VMEM((iheads * QB, 4 * HD), BF16),
                        pltpu.VMEM((QB, sk), F32)],
        compiler_params=_cparams("arbitrary", "arbitrary"),
        name=f"dsa_g{g}",
    )(*args)


def _rope_tables(seq):
    half = HEAD_DIM // 2
    inv_freq = ROPE_THETA ** (-jnp.arange(0, HEAD_DIM, 2, dtype=F32) / HEAD_DIM)
    ang = jnp.arange(seq, dtype=F32)[:, None] * inv_freq[None, :]
    cos, sin = jnp.cos(ang), jnp.sin(ang)
    cos2 = jnp.tile(cos, (1, LANES // half))
    sin2 = jnp.tile(jnp.concatenate([-sin, sin], axis=1), (1, LANES // HEAD_DIM))
    return cos2, sin2


def _block_diag(w):
    h, n, _ = w.shape
    eye = jnp.eye(h, dtype=w.dtype)
    return (eye[:, None, :, None] * w[:, :, None, :]).reshape(h * n, h * n)


def kernel(x, norm_mix, w_in0, mu_shift, w_decay0, w_decay_up, a_bias, w_a_up, w_g_up, k_k, k_a, r_k, lnx_w, lnx_b, conv_w, conv_b, w_rgate, b_rgate, w_igate, b_igate, lru_lambda, w_out0, w_in1, w_out1, norm_ffn, ffn_gate, ffn_up, ffn_down, norm_final):
    B, S, D = x.shape
    T = B * S
    HD = HEAD_DIM
    W = w_decay0.shape[-1]
    H = W // HD
    dlora, alora = w_decay_up.shape[1], w_a_up.shape[1]
    assert dlora + alora == LANES and w_g_up.shape[1] == LANES
    x2 = x.reshape(T, D)
    tm = min(512, S)

    rwkv_cols = 3 * W + 2 * LANES
    rwkv_p, lru_x, lru_gate = _norm_matmul(x2, norm_mix[0], w_in0[0].astype(BF16), (rwkv_cols, W, W), tm)
    ones_bd = _block_diag(jnp.ones((H, HD, HD), BF16))
    wdu_pad = jnp.concatenate([w_decay_up[0], jnp.zeros((alora, W), F32)], axis=0).astype(BF16)
    wau_pad = jnp.concatenate([jnp.zeros((dlora, W), F32), w_a_up[0]], axis=0).astype(BF16)
    r, w, k, v, nkk, kka, bonus, g = _rwkv_prep(
        rwkv_p, mu_shift[0], w_decay0[0], wdu_pad, a_bias[0], wau_pad, w_g_up[0].astype(BF16), k_k[0], k_a[0],
        r_k[0].reshape(-1), ones_bd, S, min(256, S))
    to_scan = lambda t: t.reshape(B, S, H, HD).transpose(1, 3, 0, 2).reshape(S, HD, B * H)
    y = _rwkv_scan(*(to_scan(t) for t in (r, w, k, v, nkk, kka)), steps=16)
    y = y.reshape(S, HD, B, H).transpose(2, 0, 3, 1).reshape(T, W)
    lru_out = _lru(lru_x, lru_gate, conv_w[0], conv_b[0], _block_diag(w_rgate[0]).astype(BF16),
                   b_rgate[0].reshape(-1), _block_diag(w_igate[0]).astype(BF16), b_igate[0].reshape(-1),
                   lru_lambda[0], B, S, min(256, S))
    x2 = _mix_out(y, bonus, g, lru_out, x2, lnx_w[0], lnx_b[0], ones_bd, w_out0[0].astype(BF16), tm)
    x2 = _ffn(x2, norm_ffn[0], ffn_gate[0].astype(BF16), ffn_up[0].astype(BF16), ffn_down[0].astype(BF16),
              norm_final, False, tm, 256)

    heads = w_out1.shape[1] // HD
    iheads = (w_in1.shape[2] - heads * HD - 3 * HD) // (HD + 1)
    c0 = heads * HD
    wq, wk, wv = w_in1[0][:, :c0], w_in1[0][:, c0:c0 + HD], w_in1[0][:, c0 + HD:c0 + 2 * HD]
    wiq = w_in1[0][:, c0 + 2 * HD:c0 + 2 * HD + iheads * HD]
    wik = w_in1[0][:, c0 + 2 * HD + iheads * HD:c0 + 3 * HD + iheads * HD]
    wiw = w_in1[0][:, c0 + 3 * HD + iheads * HD:]
    misc = 2 * LANES
    pad = jnp.zeros((D, misc - 3 * HD - iheads), F32)
    w1 = jnp.concatenate([wq, wiq, wk, wik, wv, wiw, pad], axis=1).astype(BF16)
    q, iq, kvm = _norm_matmul(x2, norm_mix[1], w1, (heads * HD, iheads * HD, misc), tm)
    cos2, sin2 = _rope_tables(S)
    topk = min(TOPK_MAX, S // 4)
    nq = S // Q_BLOCK
    gq = 4 if nq % 4 == 0 and nq > 4 else 1
    attn = jnp.zeros((T, c0), F32)
    for gi in range(nq // gq):
        attn = _dsa_group(q, iq, kvm, cos2, sin2, attn, B, S, gi, gq, topk, heads, iheads)
    x2 = _matmul_res(attn, w_out1[0].astype(BF16), x2, tm)
    x2 = _ffn(x2, norm_ffn[1], ffn_gate[1].astype(BF16), ffn_up[1].astype(BF16), ffn_down[1].astype(BF16),
              norm_final, True, tm, 256)
    return x2.reshape(B, S, D)
```

```python
import functools

import jax
import jax.numpy as jnp
from jax import lax
from jax.experimental import pallas as pl
from jax.experimental.pallas import tpu as pltpu

F32 = jnp.float32
BF16 = jnp.bfloat16
I32 = jnp.int32

NORM_EPS = 1e-6
GN_EPS = 64e-5
LRU_C = 8.0
ROPE_THETA = 10000.0
HEAD_DIM = 64
CHUNK = 64
Q_BLOCK = 128
TOPK_MAX = 256
CONV_WIDTH = 4
LANES = 128
SUBLANES = 8
VMEM_LIMIT_BYTES = 56 * 1024 * 1024
INT_MIN = -(2 ** 31)


def _cparams(*semantics):
    return pltpu.CompilerParams(dimension_semantics=semantics, vmem_limit_bytes=VMEM_LIMIT_BYTES)


def _rms(x, g):
    return x * lax.rsqrt(jnp.mean(x * x, axis=-1, keepdims=True) + NORM_EPS) * g


def _split_bf16(x, n):
    parts = []
    for _ in range(n):
        p = x.astype(BF16)
        parts.append(p)
        x = x - p.astype(F32)
    return parts


def _seg_sum(x, ones_bd):
    out = None
    for p in _split_bf16(x, 3):
        d = jnp.dot(p, ones_bd, preferred_element_type=F32)
        out = d if out is None else out + d
    return out


def _softplus(x):
    return jnp.maximum(x, 0.0) + jnp.log(1.0 + jnp.exp(-jnp.abs(x)))


def _gelu_tanh(x):
    cdf = 0.5 * (1.0 + jnp.tanh(0.7978845608028654 * (x + 0.044715 * (x * x * x))))
    return x * cdf


def _norm_matmul_kernel(x_ref, g_ref, w_ref, *o_refs, splits, n_chunk):
    h = _rms(x_ref[...], g_ref[...]).astype(BF16)
    off = 0
    for o_ref, n in zip(o_refs, splits):
        for c in range(0, n, n_chunk):
            cw = min(n_chunk, n - c)
            o_ref[:, c:c + cw] = jnp.dot(h, w_ref[:, off + c:off + c + cw], preferred_element_type=F32)
        off += n


def _norm_matmul(x, g, w, splits, tm):
    T, D = x.shape
    N = w.shape[1]
    assert sum(splits) == N and T % tm == 0
    return pl.pallas_call(
        functools.partial(_norm_matmul_kernel, splits=splits, n_chunk=512),
        grid=(T // tm,),
        in_specs=[pl.BlockSpec((tm, D), lambda i: (i, 0)),
                  pl.BlockSpec((1, D), lambda i: (0, 0)),
                  pl.BlockSpec((D, N), lambda i: (0, 0))],
        out_specs=[pl.BlockSpec((tm, n), lambda i: (i, 0)) for n in splits],
        out_shape=[jax.ShapeDtypeStruct((T, n), F32) for n in splits],
        compiler_params=_cparams("arbitrary"),
        name="norm_matmul",
    )(x, g.reshape(1, D), w)


def _rwkv_prep_kernel(p_ref, halo_ref, mu_ref, dec0_ref, wdu_ref, ab_ref, wau_ref, wgu_ref, kk_ref, ka_ref,
                      rk_ref, ones_ref, r_o, w_o, k_o, v_o, nkk_o, kka_o, bonus_o, g_o, *, tm, seq, width):
    i = pl.program_id(0)
    p = p_ref[...]
    at_seq_start = (i * tm) % seq == 0
    prev_row = jnp.where(at_seq_start, 0.0, halo_ref[SUBLANES - 1:SUBLANES, :])
    row = lax.broadcasted_iota(I32, p.shape, 0)
    prev = jnp.where(row == 0, prev_row, pltpu.roll(p, 1, axis=0))
    ps = p + (prev - p) * mu_ref[...]
    W = width
    r, k, v = ps[:, 0:W], ps[:, W:2 * W], ps[:, 2 * W:3 * W]
    lora = ps[:, 3 * W:3 * W + LANES]
    gd = ps[:, 3 * W + LANES:3 * W + 2 * LANES]
    ones_bd = ones_ref[...]

    z = dec0_ref[...] + jnp.dot(jnp.tanh(lora).astype(BF16), wdu_ref[...], preferred_element_type=F32)
    w_log = -_softplus(-z) - 0.5
    w_o[...] = jnp.exp(-jnp.exp(w_log))
    a = jax.nn.sigmoid(ab_ref[...] + jnp.dot(lora.astype(BF16), wau_ref[...], preferred_element_type=F32))
    g_o[...] = jnp.dot(jax.nn.sigmoid(gd).astype(BF16), wgu_ref[...], preferred_element_type=F32)
    kk = k * kk_ref[...]
    kk = kk * lax.rsqrt(jnp.maximum(_seg_sum(kk * kk, ones_bd), 1e-24))
    k_mod = k * (1.0 + (a - 1.0) * ka_ref[...])
    r_o[...] = r
    k_o[...] = k_mod
    v_o[...] = v
    nkk_o[...] = -kk
    kka_o[...] = kk * a
    bonus_o[...] = _seg_sum(r * k_mod * rk_ref[...], ones_bd) * v


def _rwkv_prep(rwkv_p, mu, dec0, wdu_pad, a_bias, wau_pad, wgu, k_k, k_a, r_k, ones_bd, seq, tm):
    T, P = rwkv_p.shape
    W = dec0.shape[-1]
    assert T % tm == 0 and seq % tm == 0 and tm % SUBLANES == 0
    row = lambda a: a.reshape(1, -1)
    full = lambda a: pl.BlockSpec(a.shape, lambda i: (0,) * a.ndim)
    params = [row(mu), row(dec0), wdu_pad, row(a_bias), wau_pad, wgu, row(k_k), row(k_a), row(r_k), ones_bd]
    return pl.pallas_call(
        functools.partial(_rwkv_prep_kernel, tm=tm, seq=seq, width=W),
        grid=(T // tm,),
        in_specs=[pl.BlockSpec((tm, P), lambda i: (i, 0)),
                  pl.BlockSpec((SUBLANES, P), lambda i: (jnp.maximum(i * (tm // SUBLANES) - 1, 0), 0))]
                 + [full(a) for a in params],
        out_specs=[pl.BlockSpec((tm, W), lambda i: (i, 0))] * 8,
        out_shape=[jax.ShapeDtypeStruct((T, W), F32)] * 8,
        compiler_params=_cparams("arbitrary"),
        name="rwkv_prep",
    )(rwkv_p, rwkv_p, *params)


def _rwkv_scan_kernel(r_ref, w_ref, k_ref, v_ref, nkk_ref, kka_ref, y_ref, s_ref, *, steps, n):
    @pl.when(pl.program_id(0) == 0)
    def _():
        s_ref[...] = jnp.zeros_like(s_ref)

    def step(t, carry):
        v_t = v_ref[t]
        sa = [None, None]
        for kidx in range(n):
            term = s_ref[kidx] * nkk_ref[t, kidx:kidx + 1, :]
            sa[kidx % 2] = term if sa[kidx % 2] is None else sa[kidx % 2] + term
        sa = sa[0] + sa[1]
        y = [None, None]
        for kidx in range(n):
            s_new = (s_ref[kidx] * w_ref[t, kidx:kidx + 1, :] + sa * kka_ref[t, kidx:kidx + 1, :]
                     + v_t * k_ref[t, kidx:kidx + 1, :])
            s_ref[kidx] = s_new
            term = s_new * r_ref[t, kidx:kidx + 1, :]
            y[kidx % 2] = term if y[kidx % 2] is None else y[kidx % 2] + term
        y_ref[t] = y[0] + y[1]
        return carry

    lax.fori_loop(0, steps, step, 0)


def _rwkv_scan(r, w, k, v, nkk, kka, steps):
    S, N, L = r.shape
    assert S % steps == 0
    spec = pl.BlockSpec((steps, N, L), lambda i: (i, 0, 0))
    return pl.pallas_call(
        functools.partial(_rwkv_scan_kernel, steps=steps, n=N),
        grid=(S // steps,),
        in_specs=[spec] * 6,
        out_specs=spec,
        out_shape=jax.ShapeDtypeStruct((S, N, L), F32),
        scratch_shapes=[pltpu.VMEM((N, N, L), F32)],
        compiler_params=_cparams("arbitrary"),
        name="rwkv_scan",
    )(r, w, k, v, nkk, kka)


def _lru_kernel(x_ref, halo_ref, gate_ref, cw_ref, cb_ref, wr_ref, br_ref, wi_ref, bi_ref, lam_ref, o_ref,
                h_scr, *, ts):
    j = pl.program_id(1)

    @pl.when(j == 0)
    def _():
        h_scr[...] = jnp.zeros_like(h_scr)

    x = x_ref[...]
    halo = jnp.where(j == 0, 0.0, halo_ref[...])
    row8 = lax.broadcasted_iota(I32, halo.shape, 0)
    xc = cb_ref[...] + x * cw_ref[CONV_WIDTH - 1:CONV_WIDTH, :]
    for d in range(1, CONV_WIDTH):
        rolled = pltpu.roll(x, d, axis=0)
        head = jnp.where(row8 < d, pltpu.roll(halo, d, axis=0), rolled[0:SUBLANES])
        shifted = jnp.concatenate([head, rolled[SUBLANES:]], axis=0)
        xc = xc + shifted * cw_ref[CONV_WIDTH - 1 - d:CONV_WIDTH - d, :]

    xcb = xc.astype(BF16)
    rg = jax.nn.sigmoid(jnp.dot(xcb, wr_ref[...], preferred_element_type=F32) + br_ref[...])
    ig = jax.nn.sigmoid(jnp.dot(xcb, wi_ref[...], preferred_element_type=F32) + bi_ref[...])
    log_a = -LRU_C * rg * _softplus(-lam_ref[...])
    a_cum = jnp.exp(log_a)
    th = jnp.tanh(log_a)
    x_cum = xc * ig * jnp.sqrt(-2.0 * th / (1.0 - th))

    row = lax.broadcasted_iota(I32, x.shape, 0)
    d = 1
    while d < ts:
        keep = row >= d
        a_prev = jnp.where(keep, pltpu.roll(a_cum, d, axis=0), 1.0)
        x_prev = jnp.where(keep, pltpu.roll(x_cum, d, axis=0), 0.0)
        x_cum = a_cum * x_prev + x_cum
        a_cum = a_cum * a_prev
        d *= 2
    h = x_cum + a_cum * h_scr[...]
    h_scr[...] = h[ts - 1:ts, :]
    o_ref[...] = h * _gelu_tanh(gate_ref[...])


def _lru(lru_x, lru_gate, conv_w, conv_b, wr_bd, b_r, wi_bd, b_i, lam, batch, seq, ts):
    T, W = lru_x.shape
    assert seq % ts == 0 and ts % SUBLANES == 0
    nt = seq // ts
    row = lambda a: a.reshape(1, -1)
    full = lambda a: pl.BlockSpec(a.shape, lambda b, j: (0,) * a.ndim)
    params = [conv_w, row(conv_b), wr_bd, row(b_r), wi_bd, row(b_i), row(lam)]
    return pl.pallas_call(
        functools.partial(_lru_kernel, ts=ts),
        grid=(batch, nt),
        in_specs=[pl.BlockSpec((ts, W), lambda b, j: (b * nt + j, 0)),
                  pl.BlockSpec((SUBLANES, W),
                               lambda b, j: (jnp.maximum((b * nt + j) * (ts // SUBLANES) - 1, 0), 0)),
                  pl.BlockSpec((ts, W), lambda b, j: (b * nt + j, 0))]
                 + [full(a) for a in params],
        out_specs=pl.BlockSpec((ts, W), lambda b, j: (b * nt + j, 0)),
        out_shape=jax.ShapeDtypeStruct((T, W), F32),
        scratch_shapes=[pltpu.VMEM((1, W), F32)],
        compiler_params=_cparams("arbitrary", "arbitrary"),
        name="rg_lru",
    )(lru_x, lru_x, lru_gate, *params)


def _mix_out_kernel(y_ref, bonus_ref, g_ref, lru_ref, x_ref, lw_ref, lb_ref, ones_ref, wo_ref, o_ref, *, width):
    ones_bd = ones_ref[...]
    y = y_ref[...]
    yc = y - _seg_sum(y, ones_bd) * (1.0 / HEAD_DIM)
    var = _seg_sum(yc * yc, ones_bd) * (1.0 / HEAD_DIM)
    yn = yc * lax.rsqrt(var + GN_EPS) * lw_ref[...] + lb_ref[...]
    rwkv_out = ((yn + bonus_ref[...]) * g_ref[...]).astype(BF16)
    out = x_ref[...] + jnp.dot(rwkv_out, wo_ref[0:width, :], preferred_element_type=F32)
    o_ref[...] = out + jnp.dot(lru_ref[...].astype(BF16), wo_ref[width:, :], preferred_element_type=F32)


def _mix_out(y, bonus, g, lru_out, x, lnx_w, lnx_b, ones_bd, w_out, tm):
    T, W = y.shape
    D = x.shape[1]
    row = lambda a: a.reshape(1, -1)
    full = lambda a: pl.BlockSpec(a.shape, lambda i: (0,) * a.ndim)
    params = [row(lnx_w), row(lnx_b), ones_bd, w_out]
    act = pl.BlockSpec((tm, W), lambda i: (i, 0))
    return pl.pallas_call(
        functools.partial(_mix_out_kernel, width=W),
        grid=(T // tm,),
        in_specs=[act, act, act, act, pl.BlockSpec((tm, D), lambda i: (i, 0))] + [full(a) for a in params],
        out_specs=pl.BlockSpec((tm, D), lambda i: (i, 0)),
        out_shape=jax.ShapeDtypeStruct((T, D), F32),
        compiler_params=_cparams("arbitrary"),
        name="mix_out",
    )(y, bonus, g, lru_out, x, *params)


def _matmul_res_kernel(a_ref, w_ref, x_ref, o_ref):
    o_ref[...] = x_ref[...] + jnp.dot(a_ref[...].astype(BF16), w_ref[...], preferred_element_type=F32)


def _matmul_res(a, w, x, tm):
    T, K = a.shape
    D = w.shape[1]
    return pl.pallas_call(
        _matmul_res_kernel,
        grid=(T // tm,),
        in_specs=[pl.BlockSpec((tm, K), lambda i: (i, 0)),
                  pl.BlockSpec((K, D), lambda i: (0, 0)),
                  pl.BlockSpec((tm, D), lambda i: (i, 0))],
        out_specs=pl.BlockSpec((tm, D), lambda i: (i, 0)),
        out_shape=jax.ShapeDtypeStruct((T, D), F32),
        compiler_params=_cparams("arbitrary"),
        name="matmul_res",
    )(a, w, x)


def _ffn_kernel(x_ref, g_ref, wg_ref, wu_ref, wd_ref, gf_ref, o_ref, *, th, final_norm):
    x = x_ref[...]
    h = _rms(x, g_ref[...]).astype(BF16)
    acc = x
    hidden = wg_ref.shape[1]
    for c in range(0, hidden, th):
        gt = jnp.dot(h, wg_ref[:, c:c + th], preferred_element_type=F32)
        ut = jnp.dot(h, wu_ref[:, c:c + th], preferred_element_type=F32)
        act = (gt * jax.nn.sigmoid(gt) * ut).astype(BF16)
        acc = acc + jnp.dot(act, wd_ref[c:c + th, :], preferred_element_type=F32)
    if final_norm:
        acc = _rms(acc, gf_ref[...])
    o_ref[...] = acc


def _ffn(x, g, wg, wu, wd, g_final, final_norm, tm, th):
    T, D = x.shape
    H = wg.shape[1]
    assert H % th == 0 and T % tm == 0
    resident = lambda a: pl.BlockSpec(a.shape, lambda i: (0,) * a.ndim, pipeline_mode=pl.Buffered(1))
    return pl.pallas_call(
        functools.partial(_ffn_kernel, th=th, final_norm=final_norm),
        grid=(T // tm,),
        in_specs=[pl.BlockSpec((tm, D), lambda i: (i, 0)),
                  pl.BlockSpec((1, D), lambda i: (0, 0)),
                  resident(wg), resident(wu), resident(wd),
                  pl.BlockSpec((1, D), lambda i: (0, 0))],
        out_specs=pl.BlockSpec((tm, D), lambda i: (i, 0)),
        out_shape=jax.ShapeDtypeStruct((T, D), F32),
        compiler_params=_cparams("arbitrary"),
        name="ffn",
    )(x, g.reshape(1, D), wg, wu, wd, g_final.reshape(1, D))


MASKED = -1e30
F32_MAX = 3.4028234663852886e38


def _rope(x, cos2, sin2):
    lane = lax.broadcasted_iota(I32, x.shape, 1)
    first_half = (lane % HEAD_DIM) < HEAD_DIM // 2
    partner = jnp.where(first_half, pltpu.roll(x, LANES - HEAD_DIM // 2, axis=1),
                        pltpu.roll(x, HEAD_DIM // 2, axis=1))
    return x * cos2 + partner * sin2


def _count_ge(sc_scr, thr):
    return jnp.sum(jnp.where(sc_scr[...] >= thr, 1.0, 0.0), axis=1, keepdims=True)


def _dsa_kernel(q_ref, iq_ref, qm_ref, kvm_ref, cosk_ref, sink_ref, cosq_ref, sinq_ref, prev_ref, o_ref,
                kTb_scr, ikT_scr, v_scr, qs_scr, iqs_scr, sc_scr, *, sk, q0, topk, heads, iheads, kt, hg):
    del prev_ref
    j = pl.program_id(1)
    QB, HD = Q_BLOCK, HEAD_DIM
    scale = HD ** -0.5

    @pl.when(j == 0)
    def _():
        kv = kvm_ref[...]
        kr_t = _rope(kv[:, 0:LANES], cosk_ref[...], sink_ref[...]).T
        kTb_scr[0:HD] = kr_t[0:HD].astype(BF16)
        b1, b2 = _split_bf16(kr_t[HD:2 * HD], 2)
        ikT_scr[0 * HD:1 * HD] = b1
        ikT_scr[1 * HD:2 * HD] = b1
        ikT_scr[2 * HD:3 * HD] = b2
        ikT_scr[3 * HD:4 * HD] = b2
        lane = lax.broadcasted_iota(I32, (sk, LANES), 1)
        v_scr[...] = jnp.where(lane < HD, kv[:, LANES:2 * LANES], jnp.where(lane == HD, 1.0, 0.0)).astype(BF16)
        eye = (lax.broadcasted_iota(I32, (QB, QB), 0) == lax.broadcasted_iota(I32, (QB, QB), 1)).astype(BF16)
        for h in range(heads):
            qs_scr[h * QB:(h + 1) * QB, HD:HD + QB] = eye

    cq, sq = cosq_ref[...], sinq_ref[...]
    q = q_ref[...]
    for c in range(heads // 2):
        ch = (_rope(q[:, c * LANES:(c + 1) * LANES], cq, sq) * scale).astype(BF16)
        qs_scr[(2 * c) * QB:(2 * c + 1) * QB, 0:HD] = ch[:, 0:HD]
        qs_scr[(2 * c + 1) * QB:(2 * c + 2) * QB, 0:HD] = ch[:, HD:2 * HD]
    iq = iq_ref[...]
    for c in range(iheads // 2):
        a1, a2 = _split_bf16(_rope(iq[:, c * LANES:(c + 1) * LANES], cq, sq) * scale, 2)
        for hh in range(2):
            rows = slice((2 * c + hh) * QB, (2 * c + hh + 1) * QB)
            a1h, a2h = a1[:, hh * HD:(hh + 1) * HD], a2[:, hh * HD:(hh + 1) * HD]
            iqs_scr[rows, 0 * HD:1 * HD] = a1h
            iqs_scr[rows, 1 * HD:2 * HD] = a2h
            iqs_scr[rows, 2 * HD:3 * HD] = a1h
            iqs_scr[rows, 3 * HD:4 * HD] = a2h

    iw = qm_ref[:, LANES + HD:LANES + HD + iheads] * (iheads ** -0.5)
    q_chunk = (q0 + j * QB + lax.broadcasted_iota(I32, (QB, kt), 0)) // CHUNK
    for t in range(sk // kt):
        logit = jnp.dot(iqs_scr[...], ikT_scr[:, t * kt:(t + 1) * kt], preferred_element_type=F32)
        score = None
        for h in range(iheads):
            term = iw[:, h:h + 1] * jnp.maximum(logit[h * QB:(h + 1) * QB], 0.0)
            score = term if score is None else score + term
        k_chunk = (t * kt + lax.broadcasted_iota(I32, (QB, kt), 1)) // CHUNK
        sc_scr[:, t * kt:(t + 1) * kt] = jnp.where(k_chunk <= q_chunk, score, -jnp.inf)

    s = sc_scr[...]
    n_adm = jnp.minimum((q_chunk[:, 0:1] + 1) * CHUNK, sk)
    take_all = n_adm <= topk
    rmax = jnp.max(s, axis=1, keepdims=True)
    rmin = jnp.min(jnp.where(s == -jnp.inf, jnp.inf, s), axis=1, keepdims=True)
    kf = float(topk)
    c_max = _count_ge(sc_scr, rmax)
    top_tied = c_max >= kf
    c_pos = jnp.sum(jnp.where(s > 0.0, 1.0, 0.0), axis=1, keepdims=True)
    c_nonneg = _count_ge(sc_scr, 0.0)
    zero_is_kth = (c_pos < kf) & (c_nonneg >= kf)
    from_zero = zero_is_kth | (c_pos >= kf)
    lo0 = jnp.where(top_tied, rmax, jnp.where(from_zero, 0.0, rmin))
    cnt0 = jnp.where(top_tied, c_max, jnp.where(from_zero, c_nonneg, n_adm.astype(F32)))
    closed = take_all | top_tied | zero_is_kth | (cnt0 == kf)
    hi0 = jnp.where(closed, lo0, jnp.where(c_pos >= kf, rmax, 0.0))

    def bisect(state):
        lo, hi, cnt = state
        mid = 0.5 * lo + 0.5 * hi
        inside = (mid > lo) & (mid < hi)
        c = _count_ge(sc_scr, mid)
        up = inside & (c >= kf)
        down = inside & (c < kf)
        return jnp.where(up, mid, lo), jnp.where(down, mid, hi), jnp.where(up, c, cnt)

    def pending(state):
        lo, hi, cnt = state
        mid = 0.5 * lo + 0.5 * hi
        return jnp.max(jnp.where((mid > lo) & (mid < hi) & (cnt != kf), 1.0, 0.0)) > 0.0

    lo, _, cnt = lax.while_loop(pending, lambda state: bisect(bisect(state)), (lo0, hi0, cnt0))
    thr = jnp.where(take_all, -F32_MAX, lo)
    tied = jnp.max(jnp.where(jnp.logical_not(take_all) & (cnt > kf), 1.0, 0.0)) > 0.0

    @pl.when(jnp.logical_not(tied))
    def _():
        kTb_scr[HD:HD + QB, :] = jnp.where(sc_scr[...] >= thr, 0.0, MASKED).astype(BF16)

    @pl.when(tied)
    def _():
        need = kf - jnp.sum(jnp.where(sc_scr[...] > thr, 1.0, 0.0), axis=1, keepdims=True)
        tri = (lax.broadcasted_iota(I32, (kt, kt), 0) <= lax.broadcasted_iota(I32, (kt, kt), 1)).astype(BF16)
        before = jnp.zeros((QB, 1), F32)
        for t in range(sk // kt):
            st = sc_scr[:, t * kt:(t + 1) * kt]
            eq = st == thr
            rank = before + jnp.dot(eq.astype(BF16), tri, preferred_element_type=F32)
            sel = (st > thr) | (eq & (rank <= need))
            kTb_scr[HD:HD + QB, t * kt:(t + 1) * kt] = jnp.where(sel, 0.0, MASKED).astype(BF16)
            before = rank[:, kt - 1:kt]

    R = hg * QB
    logits = lambda g: jnp.dot(qs_scr[g * R:(g + 1) * R, :], kTb_scr[...], preferred_element_type=F32)
    lg = logits(0)
    for g in range(heads // hg):
        lg_next = logits(g + 1) if g + 1 < heads // hg else None
        p = jnp.exp(lg - jnp.max(lg, axis=-1, keepdims=True)).astype(BF16)
        o = jnp.dot(p, v_scr[...], preferred_element_type=F32)
        o = o[:, 0:HD] / o[:, HD:HD + 1]
        for hh in range(hg):
            h = g * hg + hh
            o_ref[:, h * HD:(h + 1) * HD] = o[hh * QB:(hh + 1) * QB]
        lg = lg_next


def _dsa_group(q, iq, kvm, cos2, sin2, prev_out, batch, seq, g, gq, topk, heads, iheads):
    QB, HD = Q_BLOCK, HEAD_DIM
    nq = seq // QB
    sk = (g + 1) * gq * QB
    kt = min(sk, 512)
    hg = 2
    misc = kvm.shape[1]
    kvm3 = kvm.reshape(batch, seq, misc)
    qrow = lambda b, j: (b * nq + g * gq + j, 0)
    kernel = functools.partial(_dsa_kernel, sk=sk, q0=g * gq * QB, topk=topk, heads=heads, iheads=iheads,
                               kt=kt, hg=hg)
    args = [q, iq, kvm, kvm3, cos2, sin2, cos2, sin2, prev_out]
    return pl.pallas_call(
        kernel,
        grid=(batch, gq),
        in_specs=[pl.BlockSpec((QB, heads * HD), qrow),
                  pl.BlockSpec((QB, iheads * HD), qrow),
                  pl.BlockSpec((QB, misc), qrow),
                  pl.BlockSpec((None, sk, misc), lambda b, j: (b, 0, 0)),
                  pl.BlockSpec((sk, LANES), lambda b, j: (0, 0)),
                  pl.BlockSpec((sk, LANES), lambda b, j: (0, 0)),
                  pl.BlockSpec((QB, LANES), lambda b, j: (g * gq + j, 0)),
                  pl.BlockSpec((QB, LANES), lambda b, j: (g * gq + j, 0)),
                  pl.BlockSpec(memory_space=pl.ANY)],
        out_specs=pl.BlockSpec((QB, heads * HD), qrow),
        out_shape=jax.ShapeDtypeStruct((batch * seq, heads * HD), F32),
        input_output_aliases={len(args) - 1: 0},
        scratch_shapes=[pltpu.VMEM((HD + QB, sk), BF16),
                        pltpu.VMEM((4 * HD, sk), BF16),
                        pltpu.VMEM((sk, LANES), BF16),
                        pltpu.VMEM((heads * QB, HD + QB), BF16),
                        pltpu.VMEM((iheads * QB, 4 * HD), BF16),
                        pltpu.VMEM((QB, sk), F32)],
        compiler_params=_cparams("arbitrary", "arbitrary"),
        name=f"dsa_g{g}",
    )(*args)


def _rope_tables(seq):
    half = HEAD_DIM // 2
    inv_freq = ROPE_THETA ** (-jnp.arange(0, HEAD_DIM, 2, dtype=F32) / HEAD_DIM)
    ang = jnp.arange(seq, dtype=F32)[:, None] * inv_freq[None, :]
    cos, sin = jnp.cos(ang), jnp.sin(ang)
    cos2 = jnp.tile(cos, (1, LANES // half))
    sin2 = jnp.tile(jnp.concatenate([-sin, sin], axis=1), (1, LANES // HEAD_DIM))
    return cos2, sin2


def _block_diag(w):
    h, n, _ = w.shape
    eye = jnp.eye(h, dtype=w.dtype)
    return (eye[:, None, :, None] * w[:, :, None, :]).reshape(h * n, h * n)


def kernel(x, norm_mix, w_in0, mu_shift, w_decay0, w_decay_up, a_bias, w_a_up, w_g_up, k_k, k_a, r_k, lnx_w, lnx_b, conv_w, conv_b, w_rgate, b_rgate, w_igate, b_igate, lru_lambda, w_out0, w_in1, w_out1, norm_ffn, ffn_gate, ffn_up, ffn_down, norm_final):
    B, S, D = x.shape
    T = B * S
    HD = HEAD_DIM
    W = w_decay0.shape[-1]
    H = W // HD
    dlora, alora = w_decay_up.shape[1], w_a_up.shape[1]
    assert dlora + alora == LANES and w_g_up.shape[1] == LANES
    x2 = x.reshape(T, D)
    tm = min(512, S)

    rwkv_cols = 3 * W + 2 * LANES
    rwkv_p, lru_x, lru_gate = _norm_matmul(x2, norm_mix[0], w_in0[0].astype(BF16), (rwkv_cols, W, W), tm)
    ones_bd = _block_diag(jnp.ones((H, HD, HD), BF16))
    wdu_pad = jnp.concatenate([w_decay_up[0], jnp.zeros((alora, W), F32)], axis=0).astype(BF16)
    wau_pad = jnp.concatenate([jnp.zeros((dlora, W), F32), w_a_up[0]], axis=0).astype(BF16)
    r, w, k, v, nkk, kka, bonus, g = _rwkv_prep(
        rwkv_p, mu_shift[0], w_decay0[0], wdu_pad, a_bias[0], wau_pad, w_g_up[0].astype(BF16), k_k[0], k_a[0],
        r_k[0].reshape(-1), ones_bd, S, min(256, S))
    to_scan = lambda t: t.reshape(B, S, H, HD).transpose(1, 3, 0, 2).reshape(S, HD, B * H)
    y = _rwkv_scan(*(to_scan(t) for t in (r, w, k, v, nkk, kka)), steps=16)
    y = y.reshape(S, HD, B, H).transpose(2, 0, 3, 1).reshape(T, W)
    lru_out = _lru(lru_x, lru_gate, conv_w[0], conv_b[0], _block_diag(w_rgate[0]).astype(BF16),
                   b_rgate[0].reshape(-1), _block_diag(w_igate[0]).astype(BF16), b_igate[0].reshape(-1),
                   lru_lambda[0], B, S, min(256, S))
    x2 = _mix_out(y, bonus, g, lru_out, x2, lnx_w[0], lnx_b[0], ones_bd, w_out0[0].astype(BF16), tm)
    x2 = _ffn(x2, norm_ffn[0], ffn_gate[0].astype(BF16), ffn_up[0].astype(BF16), ffn_down[0].astype(BF16),
              norm_final, False, tm, 256)

    heads = w_out1.shape[1] // HD
    iheads = (w_in1.shape[2] - heads * HD - 3 * HD) // (HD + 1)
    c0 = heads * HD
    wq, wk, wv = w_in1[0][:, :c0], w_in1[0][:, c0:c0 + HD], w_in1[0][:, c0 + HD:c0 + 2 * HD]
    wiq = w_in1[0][:, c0 + 2 * HD:c0 + 2 * HD + iheads * HD]
    wik = w_in1[0][:, c0 + 2 * HD + iheads * HD:c0 + 3 * HD + iheads * HD]
    wiw = w_in1[0][:, c0 + 3 * HD + iheads * HD:]
    misc = 2 * LANES
    pad = jnp.zeros((D, misc - 3 * HD - iheads), F32)
    w1 = jnp.concatenate([wq, wiq, wk, wik, wv, wiw, pad], axis=1).astype(BF16)
    q, iq, kvm = _norm_matmul(x2, norm_mix[1], w1, (heads * HD, iheads * HD, misc), tm)
    cos2, sin2 = _rope_tables(S)
    topk = min(TOPK_MAX, S // 4)
    nq = S // Q_BLOCK
    gq = 4 if nq % 4 == 0 and nq > 4 else 1
    attn = jnp.zeros((T, c0), F32)
    for gi in range(nq // gq):
        attn = _dsa_group(q, iq, kvm, cos2, sin2, attn, B, S, gi, gq, topk, heads, iheads)
    x2 = _matmul_res(attn, w_out1[0].astype(BF16), x2, tm)
    x2 = _ffn(x2, norm_ffn[1], ffn_gate[1].astype(BF16), ffn_up[1].astype(BF16), ffn_down[1].astype(BF16),
              norm_final, True, tm, 256)
    return x2.reshape(B, S, D)
```

```python
import functools

import jax
import jax.numpy as jnp
from jax import lax
from jax.experimental import pallas as pl
from jax.experimental.pallas import tpu as pltpu

F32 = jnp.float32
BF16 = jnp.bfloat16
I32 = jnp.int32

NORM_EPS = 1e-6
GN_EPS = 64e-5
LRU_C = 8.0
ROPE_THETA = 10000.0
HEAD_DIM = 64
CHUNK = 64
Q_BLOCK = 128
TOPK_MAX = 256
CONV_WIDTH = 4
LANES = 128
SUBLANES = 8
VMEM_LIMIT_BYTES = 56 * 1024 * 1024
INT_MIN = -(2 ** 31)


def _cparams(*semantics):
    return pltpu.CompilerParams(dimension_semantics=semantics, vmem_limit_bytes=VMEM_LIMIT_BYTES)


def _rms(x, g):
    return x * lax.rsqrt(jnp.mean(x * x, axis=-1, keepdims=True) + NORM_EPS) * g


def _split_bf16(x, n):
    parts = []
    for _ in range(n):
        p = x.astype(BF16)
        parts.append(p)
        x = x - p.astype(F32)
    return parts


def _seg_sum(x, ones_bd):
    out = None
    for p in _split_bf16(x, 3):
        d = jnp.dot(p, ones_bd, preferred_element_type=F32)
        out = d if out is None else out + d
    return out


def _softplus(x):
    return jnp.maximum(x, 0.0) + jnp.log(1.0 + jnp.exp(-jnp.abs(x)))


def _gelu_tanh(x):
    cdf = 0.5 * (1.0 + jnp.tanh(0.7978845608028654 * (x + 0.044715 * (x * x * x))))
    return x * cdf


def _norm_matmul_kernel(x_ref, g_ref, w_ref, *o_refs, splits, n_chunk):
    h = _rms(x_ref[...], g_ref[...]).astype(BF16)
    off = 0
    for o_ref, n in zip(o_refs, splits):
        for c in range(0, n, n_chunk):
            cw = min(n_chunk, n - c)
            o_ref[:, c:c + cw] = jnp.dot(h, w_ref[:, off + c:off + c + cw], preferred_element_type=F32)
        off += n


def _norm_matmul(x, g, w, splits, tm):
    T, D = x.shape
    N = w.shape[1]
    assert sum(splits) == N and T % tm == 0
    return pl.pallas_call(
        functools.partial(_norm_matmul_kernel, splits=splits, n_chunk=512),
        grid=(T // tm,),
        in_specs=[pl.BlockSpec((tm, D), lambda i: (i, 0)),
                  pl.BlockSpec((1, D), lambda i: (0, 0)),
                  pl.BlockSpec((D, N), lambda i: (0, 0))],
        out_specs=[pl.BlockSpec((tm, n), lambda i: (i, 0)) for n in splits],
        out_shape=[jax.ShapeDtypeStruct((T, n), F32) for n in splits],
        compiler_params=_cparams("arbitrary"),
        name="norm_matmul",
    )(x, g.reshape(1, D), w)


def _rwkv_prep_kernel(p_ref, halo_ref, mu_ref, dec0_ref, wdu_ref, ab_ref, wau_ref, wgu_ref, kk_ref, ka_ref,
                      rk_ref, ones_ref, r_o, w_o, k_o, v_o, nkk_o, kka_o, bonus_o, g_o, *, tm, seq, width):
    i = pl.program_id(0)
    p = p_ref[...]
    at_seq_start = (i * tm) % seq == 0
    prev_row = jnp.where(at_seq_start, 0.0, halo_ref[SUBLANES - 1:SUBLANES, :])
    row = lax.broadcasted_iota(I32, p.shape, 0)
    prev = jnp.where(row == 0, prev_row, pltpu.roll(p, 1, axis=0))
    ps = p + (prev - p) * mu_ref[...]
    W = width
    r, k, v = ps[:, 0:W], ps[:, W:2 * W], ps[:, 2 * W:3 * W]
    lora = ps[:, 3 * W:3 * W + LANES]
    gd = ps[:, 3 * W + LANES:3 * W + 2 * LANES]
    ones_bd = ones_ref[...]

    z = dec0_ref[...] + jnp.dot(jnp.tanh(lora).astype(BF16), wdu_ref[...], preferred_element_type=F32)
    w_log = -_softplus(-z) - 0.5
    w_o[...] = jnp.exp(-jnp.exp(w_log))
    a = jax.nn.sigmoid(ab_ref[...] + jnp.dot(lora.astype(BF16), wau_ref[...], preferred_element_type=F32))
    g_o[...] = jnp.dot(jax.nn.sigmoid(gd).astype(BF16), wgu_ref[...], preferred_element_type=F32)
    kk = k * kk_ref[...]
    kk = kk * lax.rsqrt(jnp.maximum(_seg_sum(kk * kk, ones_bd), 1e-24))
    k_mod = k * (1.0 + (a - 1.0) * ka_ref[...])
    r_o[...] = r
    k_o[...] = k_mod
    v_o[...] = v
    nkk_o[...] = -kk
    kka_o[...] = kk * a
    bonus_o[...] = _seg_sum(r * k_mod * rk_ref[...], ones_bd) * v


def _rwkv_prep(rwkv_p, mu, dec0, wdu_pad, a_bias, wau_pad, wgu, k_k, k_a, r_k, ones_bd, seq, tm):
    T, P = rwkv_p.shape
    W = dec0.shape[-1]
    assert T % tm == 0 and seq % tm == 0 and tm % SUBLANES == 0
    row = lambda a: a.reshape(1, -1)
    full = lambda a: pl.BlockSpec(a.shape, lambda i: (0,) * a.ndim)
    params = [row(mu), row(dec0), wdu_pad, row(a_bias), wau_pad, wgu, row(k_k), row(k_a), row(r_k), ones_bd]
    nst = seq // tm
    time_major = pl.BlockSpec((tm, W), lambda i: (i % nst, i // nst))
    token_major = pl.BlockSpec((tm, W), lambda i: (i, 0))
    return pl.pallas_call(
        functools.partial(_rwkv_prep_kernel, tm=tm, seq=seq, width=W),
        grid=(T // tm,),
        in_specs=[pl.BlockSpec((tm, P), lambda i: (i, 0)),
                  pl.BlockSpec((SUBLANES, P), lambda i: (jnp.maximum(i * (tm // SUBLANES) - 1, 0), 0))]
                 + [full(a) for a in params],
        out_specs=[time_major] * 6 + [token_major] * 2,
        out_shape=[jax.ShapeDtypeStruct((seq, (T // seq) * W), F32)] * 6 + [jax.ShapeDtypeStruct((T, W), F32)] * 2,
        compiler_params=_cparams("arbitrary"),
        name="rwkv_prep",
    )(rwkv_p, rwkv_p, *params)


def _rwkv_scan_kernel(r_ref, w_ref, k_ref, v_ref, nkk_ref, kka_ref, y_ref, s_ref, rt, wt, kt, nkkt, kkat,
                      *, steps, n):
    @pl.when(pl.program_id(0) == 0)
    def _():
        s_ref[...] = jnp.zeros_like(s_ref)

    def step(t, carry):
        for src, dst in ((r_ref, rt), (w_ref, wt), (k_ref, kt), (nkk_ref, nkkt), (kka_ref, kkat)):
            dst[...] = src[t].T
        v_t = v_ref[t].T
        sa = [None, None]
        for kidx in range(n):
            term = s_ref[kidx] * nkkt[kidx:kidx + 1, :]
            sa[kidx % 2] = term if sa[kidx % 2] is None else sa[kidx % 2] + term
        sa = sa[0] + sa[1]
        y = [None, None]
        for kidx in range(n):
            s_new = (s_ref[kidx] * wt[kidx:kidx + 1, :] + sa * kkat[kidx:kidx + 1, :]
                     + v_t * kt[kidx:kidx + 1, :])
            s_ref[kidx] = s_new
            term = s_new * rt[kidx:kidx + 1, :]
            y[kidx % 2] = term if y[kidx % 2] is None else y[kidx % 2] + term
        y_ref[t] = (y[0] + y[1]).T
        return carry

    lax.fori_loop(0, steps, step, 0)


def _rwkv_scan(r, w, k, v, nkk, kka, steps):
    S, L, N = r.shape
    assert S % steps == 0
    spec = pl.BlockSpec((steps, L, N), lambda i: (i, 0, 0))
    return pl.pallas_call(
        functools.partial(_rwkv_scan_kernel, steps=steps, n=N),
        grid=(S // steps,),
        in_specs=[spec] * 6,
        out_specs=spec,
        out_shape=jax.ShapeDtypeStruct((S, L, N), F32),
        scratch_shapes=[pltpu.VMEM((N, N, L), F32)] + [pltpu.VMEM((N, L), F32)] * 5,
        compiler_params=_cparams("arbitrary"),
        name="rwkv_scan",
    )(r, w, k, v, nkk, kka)


def _lru_kernel(x_ref, halo_ref, gate_ref, cw_ref, cb_ref, wr_ref, br_ref, wi_ref, bi_ref, lam_ref, o_ref,
                h_scr, *, ts):
    j = pl.program_id(1)

    @pl.when(j == 0)
    def _():
        h_scr[...] = jnp.zeros_like(h_scr)

    x = x_ref[...]
    halo = jnp.where(j == 0, 0.0, halo_ref[...])
    row8 = lax.broadcasted_iota(I32, halo.shape, 0)
    xc = cb_ref[...] + x * cw_ref[CONV_WIDTH - 1:CONV_WIDTH, :]
    for d in range(1, CONV_WIDTH):
        rolled = pltpu.roll(x, d, axis=0)
        head = jnp.where(row8 < d, pltpu.roll(halo, d, axis=0), rolled[0:SUBLANES])
        shifted = jnp.concatenate([head, rolled[SUBLANES:]], axis=0)
        xc = xc + shifted * cw_ref[CONV_WIDTH - 1 - d:CONV_WIDTH - d, :]

    xcb = xc.astype(BF16)
    rg = jax.nn.sigmoid(jnp.dot(xcb, wr_ref[...], preferred_element_type=F32) + br_ref[...])
    ig = jax.nn.sigmoid(jnp.dot(xcb, wi_ref[...], preferred_element_type=F32) + bi_ref[...])
    log_a = -LRU_C * rg * _softplus(-lam_ref[...])
    a_cum = jnp.exp(log_a)
    th = jnp.tanh(log_a)
    x_cum = xc * ig * jnp.sqrt(-2.0 * th / (1.0 - th))

    row = lax.broadcasted_iota(I32, x.shape, 0)
    d = 1
    while d < ts:
        keep = row >= d
        a_prev = jnp.where(keep, pltpu.roll(a_cum, d, axis=0), 1.0)
        x_prev = jnp.where(keep, pltpu.roll(x_cum, d, axis=0), 0.0)
        x_cum = a_cum * x_prev + x_cum
        a_cum = a_cum * a_prev
        d *= 2
    h = x_cum + a_cum * h_scr[...]
    h_scr[...] = h[ts - 1:ts, :]
    o_ref[...] = h * _gelu_tanh(gate_ref[...])


def _lru(lru_x, lru_gate, conv_w, conv_b, wr_bd, b_r, wi_bd, b_i, lam, batch, seq, ts):
    T, W = lru_x.shape
    assert seq % ts == 0 and ts % SUBLANES == 0
    nt = seq // ts
    row = lambda a: a.reshape(1, -1)
    full = lambda a: pl.BlockSpec(a.shape, lambda b, j: (0,) * a.ndim)
    params = [conv_w, row(conv_b), wr_bd, row(b_r), wi_bd, row(b_i), row(lam)]
    return pl.pallas_call(
        functools.partial(_lru_kernel, ts=ts),
        grid=(batch, nt),
        in_specs=[pl.BlockSpec((ts, W), lambda b, j: (b * nt + j, 0)),
                  pl.BlockSpec((SUBLANES, W),
                               lambda b, j: (jnp.maximum((b * nt + j) * (ts // SUBLANES) - 1, 0), 0)),
                  pl.BlockSpec((ts, W), lambda b, j: (b * nt + j, 0))]
                 + [full(a) for a in params],
        out_specs=pl.BlockSpec((ts, W), lambda b, j: (b * nt + j, 0)),
        out_shape=jax.ShapeDtypeStruct((T, W), F32),
        scratch_shapes=[pltpu.VMEM((1, W), F32)],
        compiler_params=_cparams("arbitrary", "arbitrary"),
        name="rg_lru",
    )(lru_x, lru_x, lru_gate, *params)


def _mix_out_kernel(y_ref, bonus_ref, g_ref, lru_ref, x_ref, lw_ref, lb_ref, ones_ref, wo_ref, o_ref, *, width):
    ones_bd = ones_ref[...]
    y = y_ref[...]
    yc = y - _seg_sum(y, ones_bd) * (1.0 / HEAD_DIM)
    var = _seg_sum(yc * yc, ones_bd) * (1.0 / HEAD_DIM)
    yn = yc * lax.rsqrt(var + GN_EPS) * lw_ref[...] + lb_ref[...]
    rwkv_out = ((yn + bonus_ref[...]) * g_ref[...]).astype(BF16)
    out = x_ref[...] + jnp.dot(rwkv_out, wo_ref[0:width, :], preferred_element_type=F32)
    o_ref[...] = out + jnp.dot(lru_ref[...].astype(BF16), wo_ref[width:, :], preferred_element_type=F32)


def _mix_out(y, bonus, g, lru_out, x, lnx_w, lnx_b, ones_bd, w_out, tm):
    T, W = bonus.shape
    D = x.shape[1]
    nst = y.shape[0] // tm
    row = lambda a: a.reshape(1, -1)
    full = lambda a: pl.BlockSpec(a.shape, lambda i: (0,) * a.ndim)
    params = [row(lnx_w), row(lnx_b), ones_bd, w_out]
    act = pl.BlockSpec((tm, W), lambda i: (i, 0))
    return pl.pallas_call(
        functools.partial(_mix_out_kernel, width=W),
        grid=(T // tm,),
        in_specs=[pl.BlockSpec((tm, W), lambda i: (i % nst, i // nst)), act, act, act,
                  pl.BlockSpec((tm, D), lambda i: (i, 0))] + [full(a) for a in params],
        out_specs=pl.BlockSpec((tm, D), lambda i: (i, 0)),
        out_shape=jax.ShapeDtypeStruct((T, D), F32),
        compiler_params=_cparams("arbitrary"),
        name="mix_out",
    )(y, bonus, g, lru_out, x, *params)


def _matmul_res_kernel(a_ref, w_ref, x_ref, o_ref):
    o_ref[...] = x_ref[...] + jnp.dot(a_ref[...].astype(BF16), w_ref[...], preferred_element_type=F32)


def _matmul_res(a, w, x, tm):
    T, K = a.shape
    D = w.shape[1]
    return pl.pallas_call(
        _matmul_res_kernel,
        grid=(T // tm,),
        in_specs=[pl.BlockSpec((tm, K), lambda i: (i, 0)),
                  pl.BlockSpec((K, D), lambda i: (0, 0)),
                  pl.BlockSpec((tm, D), lambda i: (i, 0))],
        out_specs=pl.BlockSpec((tm, D), lambda i: (i, 0)),
        out_shape=jax.ShapeDtypeStruct((T, D), F32),
        compiler_params=_cparams("arbitrary"),
        name="matmul_res",
    )(a, w, x)


def _ffn_kernel(x_ref, g_ref, wg_ref, wu_ref, wd_ref, gf_ref, o_ref, *, th, final_norm):
    x = x_ref[...]
    h = _rms(x, g_ref[...]).astype(BF16)
    acc = x
    hidden = wg_ref.shape[1]
    for c in range(0, hidden, th):
        gt = jnp.dot(h, wg_ref[:, c:c + th], preferred_element_type=F32)
        ut = jnp.dot(h, wu_ref[:, c:c + th], preferred_element_type=F32)
        act = (gt * jax.nn.sigmoid(gt) * ut).astype(BF16)
        acc = acc + jnp.dot(act, wd_ref[c:c + th, :], preferred_element_type=F32)
    if final_norm:
        acc = _rms(acc, gf_ref[...])
    o_ref[...] = acc


def _ffn(x, g, wg, wu, wd, g_final, final_norm, tm, th):
    T, D = x.shape
    H = wg.shape[1]
    assert H % th == 0 and T % tm == 0
    resident = lambda a: pl.BlockSpec(a.shape, lambda i: (0,) * a.ndim, pipeline_mode=pl.Buffered(1))
    return pl.pallas_call(
        functools.partial(_ffn_kernel, th=th, final_norm=final_norm),
        grid=(T // tm,),
        in_specs=[pl.BlockSpec((tm, D), lambda i: (i, 0)),
                  pl.BlockSpec((1, D), lambda i: (0, 0)),
                  resident(wg), resident(wu), resident(wd),
                  pl.BlockSpec((1, D), lambda i: (0, 0))],
        out_specs=pl.BlockSpec((tm, D), lambda i: (i, 0)),
        out_shape=jax.ShapeDtypeStruct((T, D), F32),
        compiler_params=_cparams("arbitrary"),
        name="ffn",
    )(x, g.reshape(1, D), wg, wu, wd, g_final.reshape(1, D))


MASKED = -1e30
F32_MAX = 3.4028234663852886e38


def _rope(x, cos2, sin2):
    lane = lax.broadcasted_iota(I32, x.shape, 1)
    first_half = (lane % HEAD_DIM) < HEAD_DIM // 2
    partner = jnp.where(first_half, pltpu.roll(x, LANES - HEAD_DIM // 2, axis=1),
                        pltpu.roll(x, HEAD_DIM // 2, axis=1))
    return x * cos2 + partner * sin2


def _count_ge(sc_scr, thr):
    return jnp.sum(jnp.where(sc_scr[...] >= thr, 1.0, 0.0), axis=1, keepdims=True)


def _dsa_kernel(q_ref, iq_ref, qm_ref, kvm_ref, cosk_ref, sink_ref, cosq_ref, sinq_ref, prev_ref, o_ref,
                kTb_scr, ikT_scr, v_scr, qs_scr, iqs_scr, sc_scr, *, sk, q0, topk, heads, iheads, kt, hg):
    del prev_ref
    j = pl.program_id(1)
    QB, HD = Q_BLOCK, HEAD_DIM
    scale = HD ** -0.5

    @pl.when(j == 0)
    def _():
        kv = kvm_ref[...]
        kr_t = _rope(kv[:, 0:LANES], cosk_ref[...], sink_ref[...]).T
        kTb_scr[0:HD] = kr_t[0:HD].astype(BF16)
        b1, b2 = _split_bf16(kr_t[HD:2 * HD], 2)
        ikT_scr[0 * HD:1 * HD] = b1
        ikT_scr[1 * HD:2 * HD] = b1
        ikT_scr[2 * HD:3 * HD] = b2
        ikT_scr[3 * HD:4 * HD] = b2
        lane = lax.broadcasted_iota(I32, (sk, LANES), 1)
        v_scr[...] = jnp.where(lane < HD, kv[:, LANES:2 * LANES], jnp.where(lane == HD, 1.0, 0.0)).astype(BF16)
        eye = (lax.broadcasted_iota(I32, (QB, QB), 0) == lax.broadcasted_iota(I32, (QB, QB), 1)).astype(BF16)
        for h in range(heads):
            qs_scr[h * QB:(h + 1) * QB, HD:HD + QB] = eye

    cq, sq = cosq_ref[...], sinq_ref[...]
    q = q_ref[...]
    for c in range(heads // 2):
        ch = (_rope(q[:, c * LANES:(c + 1) * LANES], cq, sq) * scale).astype(BF16)
        qs_scr[(2 * c) * QB:(2 * c + 1) * QB, 0:HD] = ch[:, 0:HD]
        qs_scr[(2 * c + 1) * QB:(2 * c + 2) * QB, 0:HD] = ch[:, HD:2 * HD]
    iq = iq_ref[...]
    for c in range(iheads // 2):
        a1, a2 = _split_bf16(_rope(iq[:, c * LANES:(c + 1) * LANES], cq, sq) * scale, 2)
        for hh in range(2):
            rows = slice((2 * c + hh) * QB, (2 * c + hh + 1) * QB)
            a1h, a2h = a1[:, hh * HD:(hh + 1) * HD], a2[:, hh * HD:(hh + 1) * HD]
            iqs_scr[rows, 0 * HD:1 * HD] = a1h
            iqs_scr[rows, 1 * HD:2 * HD] = a2h
            iqs_scr[rows, 2 * HD:3 * HD] = a1h
            iqs_scr[rows, 3 * HD:4 * HD] = a2h

    iw = qm_ref[:, LANES + HD:LANES + HD + iheads] * (iheads ** -0.5)
    q_chunk = (q0 + j * QB + lax.broadcasted_iota(I32, (QB, kt), 0)) // CHUNK
    for t in range(sk // kt):
        logit = jnp.dot(iqs_scr[...], ikT_scr[:, t * kt:(t + 1) * kt], preferred_element_type=F32)
        score = None
        for h in range(iheads):
            term = iw[:, h:h + 1] * jnp.maximum(logit[h * QB:(h + 1) * QB], 0.0)
            score = term if score is None else score + term
        k_chunk = (t * kt + lax.broadcasted_iota(I32, (QB, kt), 1)) // CHUNK
        sc_scr[:, t * kt:(t + 1) * kt] = jnp.where(k_chunk <= q_chunk, score, -jnp.inf)

    s = sc_scr[...]
    n_adm = jnp.minimum((q_chunk[:, 0:1] + 1) * CHUNK, sk)
    take_all = n_adm <= topk
    rmax = jnp.max(s, axis=1, keepdims=True)
    rmin = jnp.min(jnp.where(s == -jnp.inf, jnp.inf, s), axis=1, keepdims=True)
    kf = float(topk)
    c_max = _count_ge(sc_scr, rmax)
    top_tied = c_max >= kf
    c_pos = jnp.sum(jnp.where(s > 0.0, 1.0, 0.0), axis=1, keepdims=True)
    c_nonneg = _count_ge(sc_scr, 0.0)
    zero_is_kth = (c_pos < kf) & (c_nonneg >= kf)
    from_zero = zero_is_kth | (c_pos >= kf)
    lo0 = jnp.where(top_tied, rmax, jnp.where(from_zero, 0.0, rmin))
    cnt0 = jnp.where(top_tied, c_max, jnp.where(from_zero, c_nonneg, n_adm.astype(F32)))
    closed = take_all | top_tied | zero_is_kth | (cnt0 == kf)
    hi0 = jnp.where(closed, lo0, jnp.where(c_pos >= kf, rmax, 0.0))

    def bisect(state):
        lo, hi, cnt = state
        mid = 0.5 * lo + 0.5 * hi
        inside = (mid > lo) & (mid < hi)
        c = _count_ge(sc_scr, mid)
        up = inside & (c >= kf)
        down = inside & (c < kf)
        return jnp.where(up, mid, lo), jnp.where(down, mid, hi), jnp.where(up, c, cnt)

    def pending(state):
        lo, hi, cnt = state
        mid = 0.5 * lo + 0.5 * hi
        return jnp.max(jnp.where((mid > lo) & (mid < hi) & (cnt != kf), 1.0, 0.0)) > 0.0

    lo, _, cnt = lax.while_loop(pending, lambda state: bisect(bisect(state)), (lo0, hi0, cnt0))
    thr = jnp.where(take_all, -F32_MAX, lo)
    tied = jnp.max(jnp.where(jnp.logical_not(take_all) & (cnt > kf), 1.0, 0.0)) > 0.0

    @pl.when(jnp.logical_not(tied))
    def _():
        kTb_scr[HD:HD + QB, :] = jnp.where(sc_scr[...] >= thr, 0.0, MASKED).astype(BF16)

    @pl.when(tied)
    def _():
        need = kf - jnp.sum(jnp.where(sc_scr[...] > thr, 1.0, 0.0), axis=1, keepdims=True)
        tri = (lax.broadcasted_iota(I32, (kt, kt), 0) <= lax.broadcasted_iota(I32, (kt, kt), 1)).astype(BF16)
        before = jnp.zeros((QB, 1), F32)
        for t in range(sk // kt):
            st = sc_scr[:, t * kt:(t + 1) * kt]
            eq = st == thr
            rank = before + jnp.dot(eq.astype(BF16), tri, preferred_element_type=F32)
            sel = (st > thr) | (eq & (rank <= need))
            kTb_scr[HD:HD + QB, t * kt:(t + 1) * kt] = jnp.where(sel, 0.0, MASKED).astype(BF16)
            before = rank[:, kt - 1:kt]

    R = hg * QB
    logits = lambda g: jnp.dot(qs_scr[g * R:(g + 1) * R, :], kTb_scr[...], preferred_element_type=F32)
    lg = logits(0)
    for g in range(heads // hg):
        lg_next = logits(g + 1) if g + 1 < heads // hg else None
        p = jnp.exp(lg - jnp.max(lg, axis=-1, keepdims=True)).astype(BF16)
        o = jnp.dot(p, v_scr[...], preferred_element_type=F32)
        o = o[:, 0:HD] / o[:, HD:HD + 1]
        for hh in range(hg):
            h = g * hg + hh
            o_ref[:, h * HD:(h + 1) * HD] = o[hh * QB:(hh + 1) * QB]
        lg = lg_next


def _dsa_group(q, iq, kvm, cos2, sin2, prev_out, batch, seq, g, gq, topk, heads, iheads):
    QB, HD = Q_BLOCK, HEAD_DIM
    nq = seq // QB
    sk = (g + 1) * gq * QB
    kt = 512 if sk % 512 == 0 else gq * QB
    assert sk % kt == 0
    hg = 2
    misc = kvm.shape[1]
    kvm3 = kvm.reshape(batch, seq, misc)
    qrow = lambda b, j: (b * nq + g * gq + j, 0)
    kernel = functools.partial(_dsa_kernel, sk=sk, q0=g * gq * QB, topk=topk, heads=heads, iheads=iheads,
                               kt=kt, hg=hg)
    args = [q, iq, kvm, kvm3, cos2, sin2, cos2, sin2, prev_out]
    return pl.pallas_call(
        kernel,
        grid=(batch, gq),
        in_specs=[pl.BlockSpec((QB, heads * HD), qrow),
                  pl.BlockSpec((QB, iheads * HD), qrow),
                  pl.BlockSpec((QB, misc), qrow),
                  pl.BlockSpec((None, sk, misc), lambda b, j: (b, 0, 0)),
                  pl.BlockSpec((sk, LANES), lambda b, j: (0, 0)),
                  pl.BlockSpec((sk, LANES), lambda b, j: (0, 0)),
                  pl.BlockSpec((QB, LANES), lambda b, j: (g * gq + j, 0)),
                  pl.BlockSpec((QB, LANES), lambda b, j: (g * gq + j, 0)),
                  pl.BlockSpec(memory_space=pl.ANY)],
        out_specs=pl.BlockSpec((QB, heads * HD), qrow),
        out_shape=jax.ShapeDtypeStruct((batch * seq, heads * HD), F32),
        input_output_aliases={len(args) - 1: 0},
        scratch_shapes=[pltpu.VMEM((HD + QB, sk), BF16),
                        pltpu.VMEM((4 * HD, sk), BF16),
                        pltpu.VMEM((sk, LANES), BF16),
                        pltpu.VMEM((heads * QB, HD + QB), BF16),
                        pltpu.VMEM((iheads * QB, 4 * HD), BF16),
                        pltpu.VMEM((QB, sk), F32)],
        compiler_params=_cparams("arbitrary", "arbitrary"),
        name=f"dsa_g{g}",
    )(*args)


def _rope_tables(seq):
    half = HEAD_DIM // 2
    inv_freq = ROPE_THETA ** (-jnp.arange(0, HEAD_DIM, 2, dtype=F32) / HEAD_DIM)
    ang = jnp.arange(seq, dtype=F32)[:, None] * inv_freq[None, :]
    cos, sin = jnp.cos(ang), jnp.sin(ang)
    cos2 = jnp.tile(cos, (1, LANES // half))
    sin2 = jnp.tile(jnp.concatenate([-sin, sin], axis=1), (1, LANES // HEAD_DIM))
    return cos2, sin2


def _block_diag(w):
    h, n, _ = w.shape
    eye = jnp.eye(h, dtype=w.dtype)
    return (eye[:, None, :, None] * w[:, :, None, :]).reshape(h * n, h * n)


def kernel(x, norm_mix, w_in0, mu_shift, w_decay0, w_decay_up, a_bias, w_a_up, w_g_up, k_k, k_a, r_k, lnx_w, lnx_b, conv_w, conv_b, w_rgate, b_rgate, w_igate, b_igate, lru_lambda, w_out0, w_in1, w_out1, norm_ffn, ffn_gate, ffn_up, ffn_down, norm_final):
    B, S, D = x.shape
    T = B * S
    HD = HEAD_DIM
    W = w_decay0.shape[-1]
    H = W // HD
    dlora, alora = w_decay_up.shape[1], w_a_up.shape[1]
    assert dlora + alora == LANES and w_g_up.shape[1] == LANES
    x2 = x.reshape(T, D)
    tm = min(512, S)

    rwkv_cols = 3 * W + 2 * LANES
    rwkv_p, lru_x, lru_gate = _norm_matmul(x2, norm_mix[0], w_in0[0].astype(BF16), (rwkv_cols, W, W), tm)
    ones_bd = _block_diag(jnp.ones((H, HD, HD), BF16))
    wdu_pad = jnp.concatenate([w_decay_up[0], jnp.zeros((alora, W), F32)], axis=0).astype(BF16)
    wau_pad = jnp.concatenate([jnp.zeros((dlora, W), F32), w_a_up[0]], axis=0).astype(BF16)
    r, w, k, v, nkk, kka, bonus, g = _rwkv_prep(
        rwkv_p, mu_shift[0], w_decay0[0], wdu_pad, a_bias[0], wau_pad, w_g_up[0].astype(BF16), k_k[0], k_a[0],
        r_k[0].reshape(-1), ones_bd, S, min(256, S))
    lru_out = _lru(lru_x, lru_gate, conv_w[0], conv_b[0], _block_diag(w_rgate[0]).astype(BF16),
                   b_rgate[0].reshape(-1), _block_diag(w_igate[0]).astype(BF16), b_igate[0].reshape(-1),
                   lru_lambda[0], B, S, min(256, S))
    to_scan = lambda t: t.reshape(S, B * H, HD)
    y = _rwkv_scan(*(to_scan(t) for t in (r, w, k, v, nkk, kka)), steps=16).reshape(S, B * W)
    x2 = _mix_out(y, bonus, g, lru_out, x2, lnx_w[0], lnx_b[0], ones_bd, w_out0[0].astype(BF16), tm)
    x2 = _ffn(x2, norm_ffn[0], ffn_gate[0].astype(BF16), ffn_up[0].astype(BF16), ffn_down[0].astype(BF16),
              norm_final, False, tm, 256)

    heads = w_out1.shape[1] // HD
    iheads = (w_in1.shape[2] - heads * HD - 3 * HD) // (HD + 1)
    c0 = heads * HD
    wq, wk, wv = w_in1[0][:, :c0], w_in1[0][:, c0:c0 + HD], w_in1[0][:, c0 + HD:c0 + 2 * HD]
    wiq = w_in1[0][:, c0 + 2 * HD:c0 + 2 * HD + iheads * HD]
    wik = w_in1[0][:, c0 + 2 * HD + iheads * HD:c0 + 3 * HD + iheads * HD]
    wiw = w_in1[0][:, c0 + 3 * HD + iheads * HD:]
    misc = 2 * LANES
    pad = jnp.zeros((D, misc - 3 * HD - iheads), F32)
    w1 = jnp.concatenate([wq, wiq, wk, wik, wv, wiw, pad], axis=1).astype(BF16)
    q, iq, kvm = _norm_matmul(x2, norm_mix[1], w1, (heads * HD, iheads * HD, misc), tm)
    cos2, sin2 = _rope_tables(S)
    topk = min(TOPK_MAX, S // 4)
    nq = S // Q_BLOCK
    gq = 2 if nq % 2 == 0 else 1
    attn = jnp.zeros((T, c0), F32)
    for gi in range(nq // gq):
        attn = _dsa_group(q, iq, kvm, cos2, sin2, attn, B, S, gi, gq, topk, heads, iheads)
    x2 = _matmul_res(attn, w_out1[0].astype(BF16), x2, tm)
    x2 = _ffn(x2, norm_ffn[1], ffn_gate[1].astype(BF16), ffn_up[1].astype(BF16), ffn_down[1].astype(BF16),
              norm_final, True, tm, 256)
    return x2.reshape(B, S, D)
```

```python
import functools

import jax
import jax.numpy as jnp
from jax import lax
from jax.experimental import pallas as pl
from jax.experimental.pallas import tpu as pltpu

F32 = jnp.float32
BF16 = jnp.bfloat16
I32 = jnp.int32

NORM_EPS = 1e-6
GN_EPS = 64e-5
LRU_C = 8.0
ROPE_THETA = 10000.0
HEAD_DIM = 64
CHUNK = 64
Q_BLOCK = 128
TOPK_MAX = 256
CONV_WIDTH = 4
LANES = 128
SUBLANES = 8
VMEM_LIMIT_BYTES = 56 * 1024 * 1024
INT_MIN = -(2 ** 31)


def _cparams(*semantics):
    return pltpu.CompilerParams(dimension_semantics=semantics, vmem_limit_bytes=VMEM_LIMIT_BYTES)


def _rms(x, g):
    return x * lax.rsqrt(jnp.mean(x * x, axis=-1, keepdims=True) + NORM_EPS) * g


def _split_bf16(x, n):
    parts = []
    for _ in range(n):
        p = x.astype(BF16)
        parts.append(p)
        x = x - p.astype(F32)
    return parts


def _seg_sum(x, ones_bd):
    out = None
    for p in _split_bf16(x, 3):
        d = jnp.dot(p, ones_bd, preferred_element_type=F32)
        out = d if out is None else out + d
    return out


def _softplus(x):
    return jnp.maximum(x, 0.0) + jnp.log(1.0 + jnp.exp(-jnp.abs(x)))


def _gelu_tanh(x):
    cdf = 0.5 * (1.0 + jnp.tanh(0.7978845608028654 * (x + 0.044715 * (x * x * x))))
    return x * cdf


def _norm_matmul_kernel(x_ref, g_ref, w_ref, *o_refs, splits, n_chunk):
    h = _rms(x_ref[...], g_ref[...]).astype(BF16)
    off = 0
    for o_ref, n in zip(o_refs, splits):
        for c in range(0, n, n_chunk):
            cw = min(n_chunk, n - c)
            o_ref[:, c:c + cw] = jnp.dot(h, w_ref[:, off + c:off + c + cw], preferred_element_type=F32)
        off += n


def _norm_matmul(x, g, w, splits, tm):
    T, D = x.shape
    N = w.shape[1]
    assert sum(splits) == N and T % tm == 0
    return pl.pallas_call(
        functools.partial(_norm_matmul_kernel, splits=splits, n_chunk=512),
        grid=(T // tm,),
        in_specs=[pl.BlockSpec((tm, D), lambda i: (i, 0)),
                  pl.BlockSpec((1, D), lambda i: (0, 0)),
                  pl.BlockSpec((D, N), lambda i: (0, 0))],
        out_specs=[pl.BlockSpec((tm, n), lambda i: (i, 0)) for n in splits],
        out_shape=[jax.ShapeDtypeStruct((T, n), F32) for n in splits],
        compiler_params=_cparams("arbitrary"),
        name="norm_matmul",
    )(x, g.reshape(1, D), w)


def _rwkv_prep_kernel(p_ref, halo_ref, mu_ref, dec0_ref, wdu_ref, ab_ref, wau_ref, wgu_ref, kk_ref, ka_ref,
                      rk_ref, ones_ref, r_o, w_o, k_o, v_o, nkk_o, kka_o, bonus_o, g_o, *, tm, seq, width):
    i = pl.program_id(0)
    p = p_ref[...]
    at_seq_start = (i * tm) % seq == 0
    prev_row = jnp.where(at_seq_start, 0.0, halo_ref[SUBLANES - 1:SUBLANES, :])
    row = lax.broadcasted_iota(I32, p.shape, 0)
    prev = jnp.where(row == 0, prev_row, pltpu.roll(p, 1, axis=0))
    ps = p + (prev - p) * mu_ref[...]
    W = width
    r, k, v = ps[:, 0:W], ps[:, W:2 * W], ps[:, 2 * W:3 * W]
    lora = ps[:, 3 * W:3 * W + LANES]
    gd = ps[:, 3 * W + LANES:3 * W + 2 * LANES]
    ones_bd = ones_ref[...]

    z = dec0_ref[...] + jnp.dot(jnp.tanh(lora).astype(BF16), wdu_ref[...], preferred_element_type=F32)
    w_log = -_softplus(-z) - 0.5
    w_o[...] = jnp.exp(-jnp.exp(w_log))
    a = jax.nn.sigmoid(ab_ref[...] + jnp.dot(lora.astype(BF16), wau_ref[...], preferred_element_type=F32))
    g_o[...] = jnp.dot(jax.nn.sigmoid(gd).astype(BF16), wgu_ref[...], preferred_element_type=F32)
    kk = k * kk_ref[...]
    kk = kk * lax.rsqrt(jnp.maximum(_seg_sum(kk * kk, ones_bd), 1e-24))
    k_mod = k * (1.0 + (a - 1.0) * ka_ref[...])
    r_o[...] = r
    k_o[...] = k_mod
    v_o[...] = v
    nkk_o[...] = -kk
    kka_o[...] = kk * a
    bonus_o[...] = _seg_sum(r * k_mod * rk_ref[...], ones_bd) * v


def _rwkv_prep(rwkv_p, mu, dec0, wdu_pad, a_bias, wau_pad, wgu, k_k, k_a, r_k, ones_bd, seq, tm):
    T, P = rwkv_p.shape
    W = dec0.shape[-1]
    assert T % tm == 0 and seq % tm == 0 and tm % SUBLANES == 0
    row = lambda a: a.reshape(1, -1)
    full = lambda a: pl.BlockSpec(a.shape, lambda i: (0,) * a.ndim)
    params = [row(mu), row(dec0), wdu_pad, row(a_bias), wau_pad, wgu, row(k_k), row(k_a), row(r_k), ones_bd]
    nst = seq // tm
    time_major = pl.BlockSpec((tm, W), lambda i: (i % nst, i // nst))
    token_major = pl.BlockSpec((tm, W), lambda i: (i, 0))
    return pl.pallas_call(
        functools.partial(_rwkv_prep_kernel, tm=tm, seq=seq, width=W),
        grid=(T // tm,),
        in_specs=[pl.BlockSpec((tm, P), lambda i: (i, 0)),
                  pl.BlockSpec((SUBLANES, P), lambda i: (jnp.maximum(i * (tm // SUBLANES) - 1, 0), 0))]
                 + [full(a) for a in params],
        out_specs=[time_major] * 6 + [token_major] * 2,
        out_shape=[jax.ShapeDtypeStruct((seq, (T // seq) * W), F32)] * 6 + [jax.ShapeDtypeStruct((T, W), F32)] * 2,
        compiler_params=_cparams("arbitrary"),
        name="rwkv_prep",
    )(rwkv_p, rwkv_p, *params)


def _rwkv_scan_kernel(r_ref, w_ref, k_ref, v_ref, nkk_ref, kka_ref, y_ref, s_ref, *, steps, n):
    @pl.when(pl.program_id(0) == 0)
    def _():
        s_ref[...] = jnp.zeros_like(s_ref)

    sa0 = [None, None]
    for kidx in range(n):
        term = s_ref[kidx] * nkk_ref[0, kidx:kidx + 1, :]
        sa0[kidx % 2] = term if sa0[kidx % 2] is None else sa0[kidx % 2] + term

    def step(t, sa):
        t_next = jnp.minimum(t + 1, steps - 1)
        v_t = v_ref[t]
        y = None
        sa_next = None
        for kidx in range(n):
            s_new = (s_ref[kidx] * w_ref[t, kidx:kidx + 1, :] + sa * kka_ref[t, kidx:kidx + 1, :]
                     + v_t * k_ref[t, kidx:kidx + 1, :])
            s_ref[kidx] = s_new
            y_term = s_new * r_ref[t, kidx:kidx + 1, :]
            sa_term = s_new * nkk_ref[t_next, kidx:kidx + 1, :]
            y = y_term if y is None else y + y_term
            sa_next = sa_term if sa_next is None else sa_next + sa_term
        y_ref[t] = y
        return sa_next

    lax.fori_loop(0, steps, step, sa0[0] + sa0[1])


def _rwkv_scan(r, w, k, v, nkk, kka, steps):
    S, N, L = r.shape
    assert S % steps == 0
    spec = pl.BlockSpec((steps, N, L), lambda i: (i, 0, 0))
    return pl.pallas_call(
        functools.partial(_rwkv_scan_kernel, steps=steps, n=N),
        grid=(S // steps,),
        in_specs=[spec] * 6,
        out_specs=spec,
        out_shape=jax.ShapeDtypeStruct((S, N, L), F32),
        scratch_shapes=[pltpu.VMEM((N, N, L), F32)],
        compiler_params=_cparams("arbitrary"),
        name="rwkv_scan",
    )(r, w, k, v, nkk, kka)


def _lru_kernel(x_ref, halo_ref, gate_ref, cw_ref, cb_ref, wr_ref, br_ref, wi_ref, bi_ref, lam_ref, o_ref,
                h_scr, *, ts):
    j = pl.program_id(1)

    @pl.when(j == 0)
    def _():
        h_scr[...] = jnp.zeros_like(h_scr)

    x = x_ref[...]
    halo = jnp.where(j == 0, 0.0, halo_ref[...])
    row8 = lax.broadcasted_iota(I32, halo.shape, 0)
    xc = cb_ref[...] + x * cw_ref[CONV_WIDTH - 1:CONV_WIDTH, :]
    for d in range(1, CONV_WIDTH):
        rolled = pltpu.roll(x, d, axis=0)
        head = jnp.where(row8 < d, pltpu.roll(halo, d, axis=0), rolled[0:SUBLANES])
        shifted = jnp.concatenate([head, rolled[SUBLANES:]], axis=0)
        xc = xc + shifted * cw_ref[CONV_WIDTH - 1 - d:CONV_WIDTH - d, :]

    xcb = xc.astype(BF16)
    rg = jax.nn.sigmoid(jnp.dot(xcb, wr_ref[...], preferred_element_type=F32) + br_ref[...])
    ig = jax.nn.sigmoid(jnp.dot(xcb, wi_ref[...], preferred_element_type=F32) + bi_ref[...])
    log_a = -LRU_C * rg * _softplus(-lam_ref[...])
    a_cum = jnp.exp(log_a)
    th = jnp.tanh(log_a)
    x_cum = xc * ig * jnp.sqrt(-2.0 * th / (1.0 - th))

    row = lax.broadcasted_iota(I32, x.shape, 0)
    d = 1
    while d < ts:
        keep = row >= d
        a_prev = jnp.where(keep, pltpu.roll(a_cum, d, axis=0), 1.0)
        x_prev = jnp.where(keep, pltpu.roll(x_cum, d, axis=0), 0.0)
        x_cum = a_cum * x_prev + x_cum
        a_cum = a_cum * a_prev
        d *= 2
    h = x_cum + a_cum * h_scr[...]
    h_scr[...] = h[ts - 1:ts, :]
    o_ref[...] = h * _gelu_tanh(gate_ref[...])


def _lru(lru_x, lru_gate, conv_w, conv_b, wr_bd, b_r, wi_bd, b_i, lam, batch, seq, ts):
    T, W = lru_x.shape
    assert seq % ts == 0 and ts % SUBLANES == 0
    nt = seq // ts
    row = lambda a: a.reshape(1, -1)
    full = lambda a: pl.BlockSpec(a.shape, lambda b, j: (0,) * a.ndim)
    params = [conv_w, row(conv_b), wr_bd, row(b_r), wi_bd, row(b_i), row(lam)]
    return pl.pallas_call(
        functools.partial(_lru_kernel, ts=ts),
        grid=(batch, nt),
        in_specs=[pl.BlockSpec((ts, W), lambda b, j: (b * nt + j, 0)),
                  pl.BlockSpec((SUBLANES, W),
                               lambda b, j: (jnp.maximum((b * nt + j) * (ts // SUBLANES) - 1, 0), 0)),
                  pl.BlockSpec((ts, W), lambda b, j: (b * nt + j, 0))]
                 + [full(a) for a in params],
        out_specs=pl.BlockSpec((ts, W), lambda b, j: (b * nt + j, 0)),
        out_shape=jax.ShapeDtypeStruct((T, W), F32),
        scratch_shapes=[pltpu.VMEM((1, W), F32)],
        compiler_params=_cparams("arbitrary", "arbitrary"),
        name="rg_lru",
    )(lru_x, lru_x, lru_gate, *params)


def _mix_out_kernel(y_ref, bonus_ref, g_ref, lru_ref, x_ref, lw_ref, lb_ref, ones_ref, wo_ref, o_ref, *, width):
    ones_bd = ones_ref[...]
    y = y_ref[...]
    yc = y - _seg_sum(y, ones_bd) * (1.0 / HEAD_DIM)
    var = _seg_sum(yc * yc, ones_bd) * (1.0 / HEAD_DIM)
    yn = yc * lax.rsqrt(var + GN_EPS) * lw_ref[...] + lb_ref[...]
    rwkv_out = ((yn + bonus_ref[...]) * g_ref[...]).astype(BF16)
    out = x_ref[...] + jnp.dot(rwkv_out, wo_ref[0:width, :], preferred_element_type=F32)
    o_ref[...] = out + jnp.dot(lru_ref[...].astype(BF16), wo_ref[width:, :], preferred_element_type=F32)


def _mix_out(y, bonus, g, lru_out, x, lnx_w, lnx_b, ones_bd, w_out, tm):
    T, W = bonus.shape
    D = x.shape[1]
    nst = y.shape[0] // tm
    row = lambda a: a.reshape(1, -1)
    full = lambda a: pl.BlockSpec(a.shape, lambda i: (0,) * a.ndim)
    params = [row(lnx_w), row(lnx_b), ones_bd, w_out]
    act = pl.BlockSpec((tm, W), lambda i: (i, 0))
    return pl.pallas_call(
        functools.partial(_mix_out_kernel, width=W),
        grid=(T // tm,),
        in_specs=[pl.BlockSpec((tm, W), lambda i: (i % nst, i // nst)), act, act, act,
                  pl.BlockSpec((tm, D), lambda i: (i, 0))] + [full(a) for a in params],
        out_specs=pl.BlockSpec((tm, D), lambda i: (i, 0)),
        out_shape=jax.ShapeDtypeStruct((T, D), F32),
        compiler_params=_cparams("arbitrary"),
        name="mix_out",
    )(y, bonus, g, lru_out, x, *params)


def _matmul_res_kernel(a_ref, w_ref, x_ref, o_ref):
    o_ref[...] = x_ref[...] + jnp.dot(a_ref[...].astype(BF16), w_ref[...], preferred_element_type=F32)


def _matmul_res(a, w, x, tm):
    T, K = a.shape
    D = w.shape[1]
    return pl.pallas_call(
        _matmul_res_kernel,
        grid=(T // tm,),
        in_specs=[pl.BlockSpec((tm, K), lambda i: (i, 0)),
                  pl.BlockSpec((K, D), lambda i: (0, 0)),
                  pl.BlockSpec((tm, D), lambda i: (i, 0))],
        out_specs=pl.BlockSpec((tm, D), lambda i: (i, 0)),
        out_shape=jax.ShapeDtypeStruct((T, D), F32),
        compiler_params=_cparams("arbitrary"),
        name="matmul_res",
    )(a, w, x)


def _ffn_kernel(x_ref, g_ref, wg_ref, wu_ref, wd_ref, gf_ref, o_ref, *, th, final_norm):
    x = x_ref[...]
    h = _rms(x, g_ref[...]).astype(BF16)
    acc = x
    hidden = wg_ref.shape[1]
    for c in range(0, hidden, th):
        gt = jnp.dot(h, wg_ref[:, c:c + th], preferred_element_type=F32)
        ut = jnp.dot(h, wu_ref[:, c:c + th], preferred_element_type=F32)
        act = (gt * jax.nn.sigmoid(gt) * ut).astype(BF16)
        acc = acc + jnp.dot(act, wd_ref[c:c + th, :], preferred_element_type=F32)
    if final_norm:
        acc = _rms(acc, gf_ref[...])
    o_ref[...] = acc


def _ffn(x, g, wg, wu, wd, g_final, final_norm, tm, th):
    T, D = x.shape
    H = wg.shape[1]
    assert H % th == 0 and T % tm == 0
    resident = lambda a: pl.BlockSpec(a.shape, lambda i: (0,) * a.ndim, pipeline_mode=pl.Buffered(1))
    return pl.pallas_call(
        functools.partial(_ffn_kernel, th=th, final_norm=final_norm),
        grid=(T // tm,),
        in_specs=[pl.BlockSpec((tm, D), lambda i: (i, 0)),
                  pl.BlockSpec((1, D), lambda i: (0, 0)),
                  resident(wg), resident(wu), resident(wd),
                  pl.BlockSpec((1, D), lambda i: (0, 0))],
        out_specs=pl.BlockSpec((tm, D), lambda i: (i, 0)),
        out_shape=jax.ShapeDtypeStruct((T, D), F32),
        compiler_params=_cparams("arbitrary"),
        name="ffn",
    )(x, g.reshape(1, D), wg, wu, wd, g_final.reshape(1, D))


MASKED = -1e30
F32_MAX = 3.4028234663852886e38


def _rope(x, cos2, sin2):
    lane = lax.broadcasted_iota(I32, x.shape, 1)
    first_half = (lane % HEAD_DIM) < HEAD_DIM // 2
    partner = jnp.where(first_half, pltpu.roll(x, LANES - HEAD_DIM // 2, axis=1),
                        pltpu.roll(x, HEAD_DIM // 2, axis=1))
    return x * cos2 + partner * sin2


def _count_ge(sc_scr, thr):
    return jnp.sum(jnp.where(sc_scr[...] >= thr, 1.0, 0.0), axis=1, keepdims=True)


def _dsa_kernel(q_ref, iq_ref, qm_ref, kvm_ref, cosk_ref, sink_ref, cosq_ref, sinq_ref, prev_ref, o_ref,
                kTb_scr, ikT_scr, v_scr, qs_scr, iqs_scr, sc_scr, *, sk, q0, topk, heads, iheads, kt, hg):
    del prev_ref
    j = pl.program_id(1)
    QB, HD = Q_BLOCK, HEAD_DIM
    scale = HD ** -0.5

    @pl.when(j == 0)
    def _():
        kv = kvm_ref[...]
        kr_t = _rope(kv[:, 0:LANES], cosk_ref[...], sink_ref[...]).T
        kTb_scr[0:HD] = kr_t[0:HD].astype(BF16)
        b1, b2 = _split_bf16(kr_t[HD:2 * HD], 2)
        ikT_scr[0 * HD:1 * HD] = b1
        ikT_scr[1 * HD:2 * HD] = b1
        ikT_scr[2 * HD:3 * HD] = b2
        ikT_scr[3 * HD:4 * HD] = b2
        lane = lax.broadcasted_iota(I32, (sk, LANES), 1)
        v_scr[...] = jnp.where(lane < HD, kv[:, LANES:2 * LANES], jnp.where(lane == HD, 1.0, 0.0)).astype(BF16)
        eye = (lax.broadcasted_iota(I32, (QB, QB), 0) == lax.broadcasted_iota(I32, (QB, QB), 1)).astype(BF16)
        for h in range(heads):
            qs_scr[h * QB:(h + 1) * QB, HD:HD + QB] = eye

    cq, sq = cosq_ref[...], sinq_ref[...]
    q = q_ref[...]
    for c in range(heads // 2):
        ch = (_rope(q[:, c * LANES:(c + 1) * LANES], cq, sq) * scale).astype(BF16)
        qs_scr[(2 * c) * QB:(2 * c + 1) * QB, 0:HD] = ch[:, 0:HD]
        qs_scr[(2 * c + 1) * QB:(2 * c + 2) * QB, 0:HD] = ch[:, HD:2 * HD]
    iq = iq_ref[...]
    for c in range(iheads // 2):
        a1, a2 = _split_bf16(_rope(iq[:, c * LANES:(c + 1) * LANES], cq, sq) * scale, 2)
        for hh in range(2):
            rows = slice((2 * c + hh) * QB, (2 * c + hh + 1) * QB)
            a1h, a2h = a1[:, hh * HD:(hh + 1) * HD], a2[:, hh * HD:(hh + 1) * HD]
            iqs_scr[rows, 0 * HD:1 * HD] = a1h
            iqs_scr[rows, 1 * HD:2 * HD] = a2h
            iqs_scr[rows, 2 * HD:3 * HD] = a1h
            iqs_scr[rows, 3 * HD:4 * HD] = a2h

    iw = qm_ref[:, LANES + HD:LANES + HD + iheads] * (iheads ** -0.5)
    q_chunk = (q0 + j * QB + lax.broadcasted_iota(I32, (QB, kt), 0)) // CHUNK
    for t in range(sk // kt):
        logit = jnp.dot(iqs_scr[...], ikT_scr[:, t * kt:(t + 1) * kt], preferred_element_type=F32)
        score = None
        for h in range(iheads):
            term = iw[:, h:h + 1] * jnp.maximum(logit[h * QB:(h + 1) * QB], 0.0)
            score = term if score is None else score + term
        k_chunk = (t * kt + lax.broadcasted_iota(I32, (QB, kt), 1)) // CHUNK
        sc_scr[:, t * kt:(t + 1) * kt] = jnp.where(k_chunk <= q_chunk, score, -jnp.inf)

    s = sc_scr[...]
    n_adm = jnp.minimum((q_chunk[:, 0:1] + 1) * CHUNK, sk)
    take_all = n_adm <= topk
    rmax = jnp.max(s, axis=1, keepdims=True)
    rmin = jnp.min(jnp.where(s == -jnp.inf, jnp.inf, s), axis=1, keepdims=True)
    kf = float(topk)
    c_max = _count_ge(sc_scr, rmax)
    top_tied = c_max >= kf
    c_pos = jnp.sum(jnp.where(s > 0.0, 1.0, 0.0), axis=1, keepdims=True)
    c_nonneg = _count_ge(sc_scr, 0.0)
    zero_is_kth = (c_pos < kf) & (c_nonneg >= kf)
    from_zero = zero_is_kth | (c_pos >= kf)
    lo0 = jnp.where(top_tied, rmax, jnp.where(from_zero, 0.0, rmin))
    cnt0 = jnp.where(top_tied, c_max, jnp.where(from_zero, c_nonneg, n_adm.astype(F32)))
    closed = take_all | top_tied | zero_is_kth | (cnt0 == kf)
    hi0 = jnp.where(closed, lo0, jnp.where(c_pos >= kf, rmax, 0.0))

    def bisect(state):
        lo, hi, cnt = state
        mid = 0.5 * lo + 0.5 * hi
        inside = (mid > lo) & (mid < hi)
        c = _count_ge(sc_scr, mid)
        up = inside & (c >= kf)
        down = inside & (c < kf)
        return jnp.where(up, mid, lo), jnp.where(down, mid, hi), jnp.where(up, c, cnt)

    def pending(state):
        lo, hi, cnt = state
        mid = 0.5 * lo + 0.5 * hi
        return jnp.max(jnp.where((mid > lo) & (mid < hi) & (cnt != kf), 1.0, 0.0)) > 0.0

    lo, _, cnt = lax.while_loop(pending, lambda state: bisect(bisect(state)), (lo0, hi0, cnt0))
    thr = jnp.where(take_all, -F32_MAX, lo)
    tied = jnp.max(jnp.where(jnp.logical_not(take_all) & (cnt > kf), 1.0, 0.0)) > 0.0

    @pl.when(jnp.logical_not(tied))
    def _():
        kTb_scr[HD:HD + QB, :] = jnp.where(sc_scr[...] >= thr, 0.0, MASKED).astype(BF16)

    @pl.when(tied)
    def _():
        need = kf - jnp.sum(jnp.where(sc_scr[...] > thr, 1.0, 0.0), axis=1, keepdims=True)
        tri = (lax.broadcasted_iota(I32, (kt, kt), 0) <= lax.broadcasted_iota(I32, (kt, kt), 1)).astype(BF16)
        before = jnp.zeros((QB, 1), F32)
        for t in range(sk // kt):
            st = sc_scr[:, t * kt:(t + 1) * kt]
            eq = st == thr
            rank = before + jnp.dot(eq.astype(BF16), tri, preferred_element_type=F32)
            sel = (st > thr) | (eq & (rank <= need))
            kTb_scr[HD:HD + QB, t * kt:(t + 1) * kt] = jnp.where(sel, 0.0, MASKED).astype(BF16)
            before = rank[:, kt - 1:kt]

    R = hg * QB
    logits = lambda g: jnp.dot(qs_scr[g * R:(g + 1) * R, :], kTb_scr[...], preferred_element_type=F32)
    lg = logits(0)
    for g in range(heads // hg):
        lg_next = logits(g + 1) if g + 1 < heads // hg else None
        p = jnp.exp(lg - jnp.max(lg, axis=-1, keepdims=True)).astype(BF16)
        o = jnp.dot(p, v_scr[...], preferred_element_type=F32)
        o = o[:, 0:HD] / o[:, HD:HD + 1]
        for hh in range(hg):
            h = g * hg + hh
            o_ref[:, h * HD:(h + 1) * HD] = o[hh * QB:(hh + 1) * QB]
        lg = lg_next


def _dsa_group(q, iq, kvm, cos2, sin2, prev_out, batch, seq, g, gq, topk, heads, iheads):
    QB, HD = Q_BLOCK, HEAD_DIM
    nq = seq // QB
    sk = (g + 1) * gq * QB
    kt = 512 if sk % 512 == 0 else gq * QB
    assert sk % kt == 0
    hg = 2
    misc = kvm.shape[1]
    kvm3 = kvm.reshape(batch, seq, misc)
    qrow = lambda b, j: (b * nq + g * gq + j, 0)
    kernel = functools.partial(_dsa_kernel, sk=sk, q0=g * gq * QB, topk=topk, heads=heads, iheads=iheads,
                               kt=kt, hg=hg)
    args = [q, iq, kvm, kvm3, cos2, sin2, cos2, sin2, prev_out]
    return pl.pallas_call(
        kernel,
        grid=(batch, gq),
        in_specs=[pl.BlockSpec((QB, heads * HD), qrow),
                  pl.BlockSpec((QB, iheads * HD), qrow),
                  pl.BlockSpec((QB, misc), qrow),
                  pl.BlockSpec((None, sk, misc), lambda b, j: (b, 0, 0)),
                  pl.BlockSpec((sk, LANES), lambda b, j: (0, 0)),
                  pl.BlockSpec((sk, LANES), lambda b, j: (0, 0)),
                  pl.BlockSpec((QB, LANES), lambda b, j: (g * gq + j, 0)),
                  pl.BlockSpec((QB, LANES), lambda b, j: (g * gq + j, 0)),
                  pl.BlockSpec(memory_space=pl.ANY)],
        out_specs=pl.BlockSpec((QB, heads * HD), qrow),
        out_shape=jax.ShapeDtypeStruct((batch * seq, heads * HD), F32),
        input_output_aliases={len(args) - 1: 0},
        scratch_shapes=[pltpu.VMEM((HD + QB, sk), BF16),
                        pltpu.VMEM((4 * HD, sk), BF16),
                        pltpu.VMEM((sk, LANES), BF16),
                        pltpu.VMEM((heads * QB, HD + QB), BF16),
                        pltpu.VMEM((iheads * QB, 4 * HD), BF16),
                        pltpu.VMEM((QB, sk), F32)],
        compiler_params=_cparams("arbitrary", "arbitrary"),
        name=f"dsa_g{g}",
    )(*args)


def _rope_tables(seq):
    half = HEAD_DIM // 2
    inv_freq = ROPE_THETA ** (-jnp.arange(0, HEAD_DIM, 2, dtype=F32) / HEAD_DIM)
    ang = jnp.arange(seq, dtype=F32)[:, None] * inv_freq[None, :]
    cos, sin = jnp.cos(ang), jnp.sin(ang)
    cos2 = jnp.tile(cos, (1, LANES // half))
    sin2 = jnp.tile(jnp.concatenate([-sin, sin], axis=1), (1, LANES // HEAD_DIM))
    return cos2, sin2


def _block_diag(w):
    h, n, _ = w.shape
    eye = jnp.eye(h, dtype=w.dtype)
    return (eye[:, None, :, None] * w[:, :, None, :]).reshape(h * n, h * n)


def kernel(x, norm_mix, w_in0, mu_shift, w_decay0, w_decay_up, a_bias, w_a_up, w_g_up, k_k, k_a, r_k, lnx_w, lnx_b, conv_w, conv_b, w_rgate, b_rgate, w_igate, b_igate, lru_lambda, w_out0, w_in1, w_out1, norm_ffn, ffn_gate, ffn_up, ffn_down, norm_final):
    B, S, D = x.shape
    T = B * S
    HD = HEAD_DIM
    W = w_decay0.shape[-1]
    H = W // HD
    dlora, alora = w_decay_up.shape[1], w_a_up.shape[1]
    assert dlora + alora == LANES and w_g_up.shape[1] == LANES
    x2 = x.reshape(T, D)
    tm = min(512, S)

    rwkv_cols = 3 * W + 2 * LANES
    rwkv_p, lru_x, lru_gate = _norm_matmul(x2, norm_mix[0], w_in0[0].astype(BF16), (rwkv_cols, W, W), tm)
    ones_bd = _block_diag(jnp.ones((H, HD, HD), BF16))
    wdu_pad = jnp.concatenate([w_decay_up[0], jnp.zeros((alora, W), F32)], axis=0).astype(BF16)
    wau_pad = jnp.concatenate([jnp.zeros((dlora, W), F32), w_a_up[0]], axis=0).astype(BF16)
    r, w, k, v, nkk, kka, bonus, g = _rwkv_prep(
        rwkv_p, mu_shift[0], w_decay0[0], wdu_pad, a_bias[0], wau_pad, w_g_up[0].astype(BF16), k_k[0], k_a[0],
        r_k[0].reshape(-1), ones_bd, S, min(256, S))
    to_scan = lambda t: t.reshape(S, B * H, HD).transpose(0, 2, 1)
    y = _rwkv_scan(*(to_scan(t) for t in (r, w, k, v, nkk, kka)), steps=16)
    y = y.transpose(0, 2, 1).reshape(S, B * W)
    lru_out = _lru(lru_x, lru_gate, conv_w[0], conv_b[0], _block_diag(w_rgate[0]).astype(BF16),
                   b_rgate[0].reshape(-1), _block_diag(w_igate[0]).astype(BF16), b_igate[0].reshape(-1),
                   lru_lambda[0], B, S, min(256, S))
    x2 = _mix_out(y, bonus, g, lru_out, x2, lnx_w[0], lnx_b[0], ones_bd, w_out0[0].astype(BF16), tm)
    x2 = _ffn(x2, norm_ffn[0], ffn_gate[0].astype(BF16), ffn_up[0].astype(BF16), ffn_down[0].astype(BF16),
              norm_final, False, tm, 256)

    heads = w_out1.shape[1] // HD
    iheads = (w_in1.shape[2] - heads * HD - 3 * HD) // (HD + 1)
    c0 = heads * HD
    wq, wk, wv = w_in1[0][:, :c0], w_in1[0][:, c0:c0 + HD], w_in1[0][:, c0 + HD:c0 + 2 * HD]
    wiq = w_in1[0][:, c0 + 2 * HD:c0 + 2 * HD + iheads * HD]
    wik = w_in1[0][:, c0 + 2 * HD + iheads * HD:c0 + 3 * HD + iheads * HD]
    wiw = w_in1[0][:, c0 + 3 * HD + iheads * HD:]
    misc = 2 * LANES
    pad = jnp.zeros((D, misc - 3 * HD - iheads), F32)
    w1 = jnp.concatenate([wq, wiq, wk, wik, wv, wiw, pad], axis=1).astype(BF16)
    q, iq, kvm = _norm_matmul(x2, norm_mix[1], w1, (heads * HD, iheads * HD, misc), tm)
    cos2, sin2 = _rope_tables(S)
    topk = min(TOPK_MAX, S // 4)
    nq = S // Q_BLOCK
    gq = 2 if nq % 2 == 0 else 1
    attn = jnp.zeros((T, c0), F32)
    for gi in range(nq // gq):
        attn = _dsa_group(q, iq, kvm, cos2, sin2, attn, B, S, gi, gq, topk, heads, iheads)
    x2 = _matmul_res(attn, w_out1[0].astype(BF16), x2, tm)
    x2 = _ffn(x2, norm_ffn[1], ffn_gate[1].astype(BF16), ffn_up[1].astype(BF16), ffn_down[1].astype(BF16),
              norm_final, True, tm, 256)
    return x2.reshape(B, S, D)
```
